```python
import jax
import jax.numpy as jnp
from jax import lax
import numpy as np

D_MODEL = 1024
BATCH = 8
SEQ = 2048
DEPTH = 4

N_EVEN = (DEPTH + 1) // 2
N_ODD = DEPTH // 2
POOL_WINDOWS = (2, 4, 8, 16)
POOL_GROUPS = 4
POOL_CH = D_MODEL // 8
POOL_WIDTH = POOL_GROUPS * POOL_CH
ATTN_HEADS = 8
HEAD_DIM = D_MODEL // 16
ATTN_WIDTH = ATTN_HEADS * HEAD_DIM
DILATED_PATTERNS = ((128, 1), (512, 4), (2048, 16))
AB_IN = POOL_WIDTH + 3 * ATTN_WIDTH
AB_OUT = POOL_WIDTH + ATTN_WIDTH
GM_CHUNK = 128
GM_GROUPS = 8
GM_WIDTH = D_MODEL
GM_CH = GM_WIDTH // GM_GROUPS
N_GROUPS = 8
EXPERTS_PER_GROUP = 8
N_EXPERTS = N_GROUPS * EXPERTS_PER_GROUP
TOP_K = 2
D_EXPERT = D_MODEL // 2
ROW_BLOCK = 128
EPS = 1e-6

kernel_name = "hybrid_pool_dilated_gmlp_hmoe"


def rms_norm(x, g):
    xf = x.astype(jnp.float32)
    y = xf * lax.rsqrt(jnp.mean(xf * xf, axis=-1, keepdims=True) + EPS) * g.astype(jnp.float32)
    return y.astype(x.dtype)


def layer_norm(x, g):
    xf = x.astype(jnp.float32)
    xc = xf - jnp.mean(xf, axis=-1, keepdims=True)
    return xc * lax.rsqrt(jnp.mean(xc * xc, axis=-1, keepdims=True) + EPS) * g.astype(jnp.float32)


def causal_mean(z, window):
    S = z.shape[1]
    c = jnp.pad(jnp.cumsum(z, axis=1), ((0, 0), (1, 0), (0, 0)))
    hi = c[:, 1:]
    lo = jnp.pad(c, ((0, 0), (window - 1, 0), (0, 0)))[:, :S]
    count = jnp.minimum(jnp.arange(S) + 1, window).astype(jnp.float32)[None, :, None]
    return (hi - lo) / count


def dilated_branch(q, k, v, window, dilation):
    B, S, H, E = q.shape
    W = window // dilation
    L = S // dilation
    nb = -(-L // W)
    Lp = nb * W

    def to_blocks(t):
        t = t.reshape(B, L, dilation, H, E).transpose(0, 2, 1, 3, 4)
        t = jnp.pad(t, ((0, 0), (0, 0), (0, Lp - L), (0, 0), (0, 0)))
        return t.reshape(B, dilation, nb, W, H, E)

    def with_prev(t):
        prev = jnp.pad(t, ((0, 0), (0, 0), (1, 0), (0, 0), (0, 0), (0, 0)))[:, :, :nb]
        return jnp.concatenate([prev, t], axis=3)

    qb = to_blocks(q)
    kk = with_prev(to_blocks(k))
    vv = with_prev(to_blocks(v))
    s = jnp.einsum('brnqhe,brnkhe->brnhqk', qb, kk) * (HEAD_DIM ** -0.5)
    qi = jnp.arange(W)[:, None]
    kj = jnp.arange(2 * W)[None, :]
    dist = qi + W - kj
    kpos = jnp.arange(nb)[:, None, None] * W + kj[None] - W
    mask = (dist >= 0)[None] & (dist <= W)[None] & (kpos >= 0)
    s = jnp.where(mask[None, None, :, None], s, -jnp.inf)
    m = jnp.max(s, axis=-1, keepdims=True)
    p = jnp.exp(s - m)
    den = jnp.sum(p, axis=-1)
    o = jnp.einsum('brnhqk,brnkhe->brnqhe', p, vv) / jnp.swapaxes(den, 3, 4)[..., None]
    lse = jnp.swapaxes(m[..., 0] + jnp.log(den), 3, 4)

    def from_blocks(t):
        rest = t.shape[4:]
        t = t.reshape((B, dilation, Lp) + rest)[:, :, :L]
        t = jnp.moveaxis(t, 2, 1)
        return t.reshape((B, S) + rest)

    return from_blocks(o), from_blocks(lse)


def dilated_attention(q, k, v):
    outs, lses = [], []
    for window, dilation in DILATED_PATTERNS:
        o, l = dilated_branch(q, k, v, window, dilation)
        outs.append(o)
        lses.append(l)
    wts = jax.nn.softmax(jnp.stack(lses, axis=0), axis=0)
    return jnp.sum(wts[..., None] * jnp.stack(outs, axis=0), axis=0)


def pool_attn_mixer(h, w_in, w_out, pool_w, pool_scale):
    B, S, _ = h.shape
    z = h @ w_in
    zp = z[..., :POOL_WIDTH]
    q = z[..., POOL_WIDTH:POOL_WIDTH + ATTN_WIDTH]
    k = z[..., POOL_WIDTH + ATTN_WIDTH:POOL_WIDTH + 2 * ATTN_WIDTH]
    v = z[..., POOL_WIDTH + 2 * ATTN_WIDTH:]
    zpf = zp.astype(jnp.float32).reshape(B, S, POOL_GROUPS, POOL_CH)
    pooled = jnp.stack([causal_mean(zpf[:, :, g], w) for g, w in enumerate(POOL_WINDOWS)], axis=2) - zpf
    a = jnp.einsum('bsgc,gcd->bsgd', pooled, pool_w).reshape(B, S, POOL_WIDTH) * pool_scale
    to_heads = lambda t: t.astype(jnp.float32).reshape(B, S, ATTN_HEADS, HEAD_DIM)
    o = dilated_attention(to_heads(q), to_heads(k), to_heads(v)).reshape(B, S, ATTN_WIDTH)
    y = jnp.concatenate([a, o], axis=-1).astype(h.dtype)
    return y @ w_out


def gmlp_mixer(h, w_in, norm_g, ws, bs, w_out):
    B, S, _ = h.shape
    z = jax.nn.gelu(h @ w_in)
    u = z[..., :GM_WIDTH].astype(jnp.float32)
    v = layer_norm(z[..., GM_WIDTH:], norm_g)
    vc = v.reshape(B, S // GM_CHUNK, GM_CHUNK, GM_GROUPS, GM_CH)
    tril = jnp.tril(jnp.ones((GM_CHUNK, GM_CHUNK), jnp.float32))
    ws_c = ws.astype(jnp.float32) * tril[None]
    sv = jnp.einsum('gij,bnjgc->bnigc', ws_c, vc) + bs.astype(jnp.float32).T[None, None, :, :, None]
    y = (u * sv.reshape(B, S, GM_WIDTH)).astype(h.dtype)
    return y @ w_out


def hierarchical_moe(h, group_w, group_b, exp_w, exp_b, w_gate, w_up, w_down):
    N, D = h.shape
    g_prob = jax.nn.softmax((h @ group_w).astype(jnp.float32) + group_b, axis=-1)
    g_top_p, g_idx = lax.top_k(g_prob, 1)
    e_logits = ((h @ exp_w).astype(jnp.float32) + exp_b).reshape(N, N_GROUPS, EXPERTS_PER_GROUP)
    e_logits = jnp.take_along_axis(e_logits, g_idx[:, :, None], axis=1)[:, 0]
    e_top, e_idx = lax.top_k(e_logits, TOP_K)
    gate = g_top_p * jax.nn.softmax(e_top, axis=-1)
    expert = g_idx * EXPERTS_PER_GROUP + e_idx

    NK = N * TOP_K
    flat_e = expert.reshape(-1)
    flat_tok = jnp.repeat(jnp.arange(N, dtype=jnp.int32), TOP_K)
    flat_gate = gate.reshape(-1)
    order = jnp.argsort(flat_e)
    se = flat_e[order]
    counts = jnp.bincount(flat_e, length=N_EXPERTS)
    starts = jnp.cumsum(counts) - counts
    pcounts = (counts + ROW_BLOCK - 1) // ROW_BLOCK * ROW_BLOCK
    pends = jnp.cumsum(pcounts)
    pstarts = pends - pcounts
    dest = pstarts[se] + jnp.arange(NK) - starts[se]
    n_rows = (NK + N_EXPERTS * (ROW_BLOCK - 1) + ROW_BLOCK - 1) // ROW_BLOCK * ROW_BLOCK
    n_blk = n_rows // ROW_BLOCK
    row_tok = jnp.full((n_rows,), N, jnp.int32).at[dest].set(flat_tok[order])
    row_gate = jnp.zeros((n_rows,), jnp.float32).at[dest].set(flat_gate[order])
    blk_expert = jnp.minimum(jnp.searchsorted(pends, jnp.arange(n_blk) * ROW_BLOCK, side='right'),
                             N_EXPERTS - 1)
    h_pad = jnp.concatenate([h, jnp.zeros((1, D), h.dtype)], axis=0)
    xb = h_pad[row_tok].reshape(n_blk, ROW_BLOCK, D)

    def expert_block(args):
        xblk, e = args
        a = xblk @ w_gate[e]
        b = xblk @ w_up[e]
        return (jax.nn.silu(a) * b) @ w_down[e]

    yb = lax.map(expert_block, (xb, blk_expert)).reshape(n_rows, D)
    out = jnp.zeros((N + 1, D), jnp.float32).at[row_tok].add(yb.astype(jnp.float32) * row_gate[:, None])
    return out[:N].astype(h.dtype)


def setup_inputs(seed: int = 0) -> dict:
    key = jax.random.key(seed)
    ks = jax.random.split(key, 20)
    f32 = jnp.float32
    D = D_MODEL

    def nrm(k, shape, scale):
        return jax.random.normal(k, shape, f32) * scale

    return {
        "x": nrm(ks[0], (BATCH, SEQ, D), 1.0),
        "norm_mix": 1.0 + nrm(ks[1], (DEPTH, D), 0.05),
        "norm_ffn": 1.0 + nrm(ks[2], (DEPTH, D), 0.05),
        "norm_final": 1.0 + nrm(ks[3], (D,), 0.05),
        "ab_w_in": nrm(ks[4], (N_EVEN, D, AB_IN), D ** -0.5),
        "ab_w_out": nrm(ks[5], (N_EVEN, AB_OUT, D), AB_OUT ** -0.5),
        "pool_w": nrm(ks[6], (N_EVEN, POOL_GROUPS, POOL_CH, POOL_CH), POOL_CH ** -0.5),
        "pool_scale": 1.0 + nrm(ks[7], (N_EVEN, POOL_WIDTH), 0.1),
        "gm_w_in": nrm(ks[8], (N_ODD, D, 2 * GM_WIDTH), D ** -0.5),
        "gm_norm": 1.0 + nrm(ks[9], (N_ODD, GM_WIDTH), 0.05),
        "gm_ws": nrm(ks[10], (N_ODD, GM_GROUPS, GM_CHUNK, GM_CHUNK), GM_CHUNK ** -0.5),
        "gm_bs": 1.0 + nrm(ks[11], (N_ODD, GM_GROUPS, GM_CHUNK), 0.1),
        "gm_w_out": nrm(ks[12], (N_ODD, GM_WIDTH, D), GM_WIDTH ** -0.5),
        "router_group_w": nrm(ks[13], (DEPTH, D, N_GROUPS), D ** -0.5),
        "router_group_b": nrm(ks[14], (DEPTH, N_GROUPS), 0.01),
        "router_expert_w": nrm(ks[15], (DEPTH, D, N_EXPERTS), D ** -0.5),
        "router_expert_b": nrm(ks[16], (DEPTH, N_EXPERTS), 0.01),
        "moe_w_gate": nrm(ks[17], (DEPTH, N_EXPERTS, D, D_EXPERT), D ** -0.5),
        "moe_w_up": nrm(ks[18], (DEPTH, N_EXPERTS, D, D_EXPERT), D ** -0.5),
        "moe_w_down": nrm(ks[19], (DEPTH, N_EXPERTS, D_EXPERT, D), D_EXPERT ** -0.5),
    }


def reference(x, norm_mix, norm_ffn, norm_final, ab_w_in, ab_w_out, pool_w, pool_scale,
              gm_w_in, gm_norm, gm_ws, gm_bs, gm_w_out, router_group_w, router_group_b,
              router_expert_w, router_expert_b, moe_w_gate, moe_w_up, moe_w_down):
    B, S, D = x.shape
    for layer in range(DEPTH):
        i = layer // 2
        h = rms_norm(x, norm_mix[layer])
        if layer % 2 == 0:
            x = x + pool_attn_mixer(h, ab_w_in[i], ab_w_out[i], pool_w[i], pool_scale[i])
        else:
            x = x + gmlp_mixer(h, gm_w_in[i], gm_norm[i], gm_ws[i], gm_bs[i], gm_w_out[i])
        h = rms_norm(x, norm_ffn[layer]).reshape(B * S, D)
        x = x + hierarchical_moe(h, router_group_w[layer], router_group_b[layer],
                                 router_expert_w[layer], router_expert_b[layer],
                                 moe_w_gate[layer], moe_w_up[layer], moe_w_down[layer]).reshape(B, S, D)
    return rms_norm(x, norm_final)
```

```python
import functools

import jax
import jax.numpy as jnp
from jax import lax
from jax.experimental import pallas as pl
from jax.experimental.pallas import tpu as pltpu

D_MODEL = 1024
BATCH = 8
SEQ = 2048
DEPTH = 4
N_TOK = BATCH * SEQ

POOL_WINDOWS = (2, 4, 8, 16)
POOL_CH = 128
POOL_WIDTH = 512
POOL_HALO = 16
HEAD_DIM = 64
ATTN_WIDTH = 512
HEADS_PER_STEP = 2
DILATIONS = (1, 4, 16)
ATTN_BLOCK = 128
GM_CHUNK = 128
GM_GROUPS = 8
GM_WIDTH = 1024
N_GROUPS = 8
EXPERTS_PER_GROUP = 8
N_EXPERTS = 64
TOP_K = 2
D_EXPERT = 512
ROW_BLOCK = 128
N_ROWS = (N_TOK * TOP_K + N_EXPERTS * (ROW_BLOCK - 1) + ROW_BLOCK - 1) // ROW_BLOCK * ROW_BLOCK
N_BLK = N_ROWS // ROW_BLOCK
EPS = 1e-6
NEG = -1e30

LANES = 128
V7X_VMEM_BYTES = 64 * 1024 * 1024
MIB = 1024 * 1024

F32 = jnp.float32
BF16 = jnp.bfloat16


def _params(semantics, vmem_mib):
    assert vmem_mib * MIB < V7X_VMEM_BYTES
    return pltpu.CompilerParams(dimension_semantics=semantics, vmem_limit_bytes=vmem_mib * MIB)


def _rms(x, g):
    return x * lax.rsqrt(jnp.mean(x * x, axis=-1, keepdims=True) + EPS) * g


def _dot(a, b):
    return jnp.dot(a, b, preferred_element_type=F32)


def _dot_nt(a, b):
    return lax.dot_general(a, b, (((1,), (1,)), ((), ())), preferred_element_type=F32)


AB_TM = 512


def _ab_in_body(x_ref, g_ref, w_ref, zp_ref, qkv_ref):
    h = _rms(x_ref[...], g_ref[...]).astype(BF16)
    z = _dot(h, w_ref[...])
    zp_ref[...] = z[:, :POOL_WIDTH]
    q = z[:, POOL_WIDTH:POOL_WIDTH + ATTN_WIDTH] * (HEAD_DIM ** -0.5)
    qkv_ref[...] = jnp.concatenate([q, z[:, POOL_WIDTH + ATTN_WIDTH:]], axis=1).astype(BF16)


def _ab_in(x, g, w_bf16):
    n = x.shape[0]
    return pl.pallas_call(
        _ab_in_body,
        grid=(n // AB_TM,),
        in_specs=[
            pl.BlockSpec((AB_TM, D_MODEL), lambda i: (i, 0)),
            pl.BlockSpec((1, D_MODEL), lambda i: (0, 0)),
            pl.BlockSpec((D_MODEL, 4 * ATTN_WIDTH), lambda i: (0, 0)),
        ],
        out_specs=[
            pl.BlockSpec((AB_TM, POOL_WIDTH), lambda i: (i, 0)),
            pl.BlockSpec((AB_TM, 3 * ATTN_WIDTH), lambda i: (i, 0)),
        ],
        out_shape=[
            jax.ShapeDtypeStruct((n, POOL_WIDTH), F32),
            jax.ShapeDtypeStruct((n, 3 * ATTN_WIDTH), BF16),
        ],
        compiler_params=_params(("arbitrary",), 40),
        name="ab_in",
    )(x, g, w_bf16)


def _attn_body(q_ref, k_ref, v_ref, o_ref, qf, kf, vf, o1, o4, o16, l1, l4, l16):
    qf[...] = q_ref[...].astype(F32)
    kf[...] = k_ref[...].astype(F32)
    vf[...] = v_ref[...].astype(F32)

    qi = lax.broadcasted_iota(jnp.int32, (ATTN_BLOCK, ATTN_BLOCK), 0)
    kj = lax.broadcasted_iota(jnp.int32, (ATTN_BLOCK, ATTN_BLOCK), 1)
    cur_ok = kj <= qi
    prev_ok = kj >= qi

    def rows(ref, start, d):
        if d == 1:
            return ref[pl.ds(start, ATTN_BLOCK), :]
        return ref[pl.ds(start, ATTN_BLOCK, stride=d), :]

    def put(ref, start, d, val):
        if d == 1:
            ref[pl.ds(start, ATTN_BLOCK), :] = val
        else:
            ref[pl.ds(start, ATTN_BLOCK, stride=d), :] = val

    for d, o_acc, l_acc in ((1, o1, l1), (4, o4, l4), (16, o16, l16)):
        nb = SEQ // d // ATTN_BLOCK
        has_prev = nb > 1

        def block(idx, carry, d=d, nb=nb, has_prev=has_prev, o_acc=o_acc, l_acc=l_acc):
            r = idx // nb
            n = idx % nb
            start = r + n * (ATTN_BLOCK * d)
            q = rows(qf, start, d).astype(BF16)
            kc = rows(kf, start, d).astype(BF16)
            vc = rows(vf, start, d).astype(BF16)
            if has_prev:
                pstart = r + jnp.maximum(n - 1, 0) * (ATTN_BLOCK * d)
                kp = rows(kf, pstart, d).astype(BF16)
                vp = rows(vf, pstart, d).astype(BF16)
                p_ok = jnp.logical_and(prev_ok, n > 0)
            outs, lses = [], []
            for hh in range(HEADS_PER_STEP):
                sl = slice(hh * HEAD_DIM, (hh + 1) * HEAD_DIM)
                s_c = _dot_nt(q[:, sl], kc[:, sl])
                m = jnp.max(jnp.where(cur_ok, s_c, NEG), axis=1, keepdims=True)
                if has_prev:
                    s_p = _dot_nt(q[:, sl], kp[:, sl])
                    m = jnp.maximum(m, jnp.max(jnp.where(p_ok, s_p, NEG), axis=1, keepdims=True))
                p_c = jnp.where(cur_ok, jnp.exp(s_c - m), 0.0)
                den = jnp.sum(p_c, axis=1, keepdims=True)
                acc = _dot(p_c.astype(BF16), vc[:, sl])
                if has_prev:
                    p_p = jnp.where(p_ok, jnp.exp(s_p - m), 0.0)
                    den = den + jnp.sum(p_p, axis=1, keepdims=True)
                    acc = acc + _dot(p_p.astype(BF16), vp[:, sl])
                outs.append(acc / den)
                lses.append(jnp.broadcast_to(m + jnp.log(den), (ATTN_BLOCK, HEAD_DIM)))
            put(o_acc, start, d, jnp.concatenate(outs, axis=1))
            put(l_acc, start, d, jnp.concatenate(lses, axis=1))
            return carry

        lax.fori_loop(0, SEQ // ATTN_BLOCK, block, 0)

    def merge(c, carry):
        sl = pl.ds(pl.multiple_of(c * ATTN_BLOCK, ATTN_BLOCK), ATTN_BLOCK)
        la, lb, lc = l1[sl, :], l4[sl, :], l16[sl, :]
        mx = jnp.maximum(jnp.maximum(la, lb), lc)
        wa, wb, wc = jnp.exp(la - mx), jnp.exp(lb - mx), jnp.exp(lc - mx)
        num = wa * o1[sl, :] + wb * o4[sl, :] + wc * o16[sl, :]
        o_ref[sl, :] = (num / (wa + wb + wc)).astype(o_ref.dtype)
        return carry

    lax.fori_loop(0, SEQ // ATTN_BLOCK, merge, 0)


def _attention(qkv):
    n_hp = ATTN_WIDTH // LANES
    blk = (None, SEQ, LANES)
    return pl.pallas_call(
        _attn_body,
        grid=(BATCH, n_hp),
        in_specs=[
            pl.BlockSpec(blk, lambda b, hp: (b, 0, hp)),
            pl.BlockSpec(blk, lambda b, hp: (b, 0, n_hp + hp)),
            pl.BlockSpec(blk, lambda b, hp: (b, 0, 2 * n_hp + hp)),
        ],
        out_specs=pl.BlockSpec(blk, lambda b, hp: (b, 0, hp)),
        out_shape=jax.ShapeDtypeStruct((BATCH, SEQ, ATTN_WIDTH), BF16),
        scratch_shapes=[pltpu.VMEM((SEQ, LANES), F32) for _ in range(9)],
        compiler_params=_params(("arbitrary", "arbitrary"), 32),
        name="dilated_attn",
    )(qkv, qkv, qkv)


AO_TS = 512


def _ab_out_body(zp_ref, halo_ref, o_ref, x_ref, pw_ref, ps_ref, w_ref, out_ref, zz):
    i = pl.program_id(1)
    zz[0:POOL_HALO, :] = jnp.where(i > 0, halo_ref[...], 0.0)
    zz[POOL_HALO:, :] = zp_ref[...]
    pos = i * AO_TS + lax.broadcasted_iota(jnp.int32, (AO_TS, 1), 0)
    parts = []
    for g, w in enumerate(POOL_WINDOWS):
        cols = slice(g * POOL_CH, (g + 1) * POOL_CH)
        acc = zz[pl.ds(POOL_HALO, AO_TS), cols]
        for j in range(1, w):
            acc = acc + zz[pl.ds(POOL_HALO - j, AO_TS), cols]
        count = jnp.minimum(pos + 1, w).astype(F32)
        pooled = acc / count - zp_ref[:, cols]
        parts.append(_dot(pooled.astype(BF16), pw_ref[g]))
    a = jnp.concatenate(parts, axis=1) * ps_ref[...]
    y = jnp.concatenate([a.astype(BF16), o_ref[...]], axis=1)
    out_ref[...] = x_ref[...] + _dot(y, w_ref[...])


def _ab_out(zp, o, x, pool_w_bf16, pool_scale, w_out_bf16):
    halo_per_tile = AO_TS // POOL_HALO
    return pl.pallas_call(
        _ab_out_body,
        grid=(BATCH, SEQ // AO_TS),
        in_specs=[
            pl.BlockSpec((None, AO_TS, POOL_WIDTH), lambda b, i: (b, i, 0)),
            pl.BlockSpec((None, POOL_HALO, POOL_WIDTH),
                         lambda b, i: (b, jnp.maximum(i * halo_per_tile - 1, 0), 0)),
            pl.BlockSpec((None, AO_TS, ATTN_WIDTH), lambda b, i: (b, i, 0)),
            pl.BlockSpec((None, AO_TS, D_MODEL), lambda b, i: (b, i, 0)),
            pl.BlockSpec((len(POOL_WINDOWS), POOL_CH, POOL_CH), lambda b, i: (0, 0, 0)),
            pl.BlockSpec((1, POOL_WIDTH), lambda b, i: (0, 0)),
            pl.BlockSpec((POOL_WIDTH + ATTN_WIDTH, D_MODEL), lambda b, i: (0, 0)),
        ],
        out_specs=pl.BlockSpec((None, AO_TS, D_MODEL), lambda b, i: (b, i, 0)),
        out_shape=jax.ShapeDtypeStruct((BATCH, SEQ, D_MODEL), F32),
        scratch_shapes=[pltpu.VMEM((AO_TS + POOL_HALO, POOL_WIDTH), F32)],
        compiler_params=_params(("arbitrary", "arbitrary"), 32),
        name="pool_ab_out",
    )(zp, zp, o, x, pool_w_bf16, pool_scale, w_out_bf16)


GM_TS = 256


def _gelu_tanh(x):
    return 0.5 * x * (1.0 + jnp.tanh(0.7978845608028654 * (x + 0.044715 * (x * x * x))))


def _gmlp_body(x_ref, g_ref, win_ref, gn_ref, ws_ref, bs_ref, wout_ref, out_ref):
    x = x_ref[...]
    h = _rms(x, g_ref[...]).astype(BF16)
    z = _gelu_tanh(_dot(h, win_ref[...]))
    u = z[:, :GM_WIDTH]
    v = z[:, GM_WIDTH:]
    vc = v - jnp.mean(v, axis=-1, keepdims=True)
    vn = (vc * lax.rsqrt(jnp.mean(vc * vc, axis=-1, keepdims=True) + EPS) * gn_ref[...]).astype(BF16)
    ri = lax.broadcasted_iota(jnp.int32, (GM_CHUNK, GM_CHUNK), 0)
    ci = lax.broadcasted_iota(jnp.int32, (GM_CHUNK, GM_CHUNK), 1)
    causal = ci <= ri
    rows = []
    for c in range(GM_TS // GM_CHUNK):
        rsl = slice(c * GM_CHUNK, (c + 1) * GM_CHUNK)
        cols = []
        for g in range(GM_GROUPS):
            csl = slice(g * (GM_WIDTH // GM_GROUPS), (g + 1) * (GM_WIDTH // GM_GROUPS))
            ws_c = jnp.where(causal, ws_ref[g], 0.0).astype(BF16)
            sv = _dot(ws_c, vn[rsl, csl]) + bs_ref[g]
            cols.append(u[rsl, csl] * sv)
        rows.append(jnp.concatenate(cols, axis=1))
    y = jnp.concatenate(rows, axis=0).astype(BF16)
    out_ref[...] = x + _dot(y, wout_ref[...])


def _gmlp(x, g, w_in_bf16, gm_norm, ws, bs_bcast, w_out_bf16):
    n = x.shape[0]
    const2 = lambda i: (0, 0)
    const3 = lambda i: (0, 0, 0)
    return pl.pallas_call(
        _gmlp_body,
        grid=(n // GM_TS,),
        in_specs=[
            pl.BlockSpec((GM_TS, D_MODEL), lambda i: (i, 0)),
            pl.BlockSpec((1, D_MODEL), const2),
            pl.BlockSpec((D_MODEL, 2 * GM_WIDTH), const2),
            pl.BlockSpec((1, GM_WIDTH), const2),
            pl.BlockSpec((GM_GROUPS, GM_CHUNK, GM_CHUNK), const3),
            pl.BlockSpec((GM_GROUPS, GM_CHUNK, GM_CHUNK), const3),
            pl.BlockSpec((GM_WIDTH, D_MODEL), const2),
        ],
        out_specs=pl.BlockSpec((GM_TS, D_MODEL), lambda i: (i, 0)),
        out_shape=jax.ShapeDtypeStruct((n, D_MODEL), F32),
        compiler_params=_params(("arbitrary",), 40),
        name="gmlp",
    )(x, g, w_in_bf16, gm_norm, ws, bs_bcast, w_out_bf16)


RT_TN = 512
ROUTER_LANES = LANES
EXPERT_LANE0 = N_GROUPS


def _router_body(x_ref, g_ref, wr_ref, br_ref, h_ref, ri_ref, rg_ref, cnt_ref, tri, running):
    step = pl.program_id(0)

    @pl.when(step == 0)
    def _():
        r = lax.broadcasted_iota(jnp.int32, (RT_TN, RT_TN), 0)
        c = lax.broadcasted_iota(jnp.int32, (RT_TN, RT_TN), 1)
        tri[...] = jnp.where(c < r, 1.0, 0.0).astype(BF16)
        running[...] = jnp.zeros_like(running)

    h = _rms(x_ref[...], g_ref[...])
    h_ref[...] = h
    logits = jnp.dot(h, wr_ref[...], preferred_element_type=F32,
                     precision=lax.Precision.HIGHEST) + br_ref[...]
    lane = lax.broadcasted_iota(jnp.int32, (RT_TN, ROUTER_LANES), 1)

    is_group = lane < N_GROUPS
    lg = jnp.where(is_group, logits, NEG)
    mg = jnp.max(lg, axis=1, keepdims=True)
    p_top = 1.0 / jnp.sum(jnp.where(is_group, jnp.exp(logits - mg), 0.0), axis=1, keepdims=True)
    g_idx = jnp.min(jnp.where(lg == mg, lane, ROUTER_LANES), axis=1, keepdims=True)

    in_group = jnp.logical_and(
        jnp.logical_and(lane >= EXPERT_LANE0, lane < EXPERT_LANE0 + N_EXPERTS),
        ((lane - EXPERT_LANE0) >> 3) == g_idx)
    le = jnp.where(in_group, logits, NEG)
    t1 = jnp.max(le, axis=1, keepdims=True)
    i1 = jnp.min(jnp.where(le == t1, lane, ROUTER_LANES), axis=1, keepdims=True)
    le2 = jnp.where(lane == i1, NEG, le)
    t2 = jnp.max(le2, axis=1, keepdims=True)
    i2 = jnp.min(jnp.where(le2 == t2, lane, ROUTER_LANES), axis=1, keepdims=True)
    e0 = i1 - EXPERT_LANE0
    e1 = i2 - EXPERT_LANE0
    ex = jnp.exp(t2 - t1)
    w0 = 1.0 / (1.0 + ex)
    g0 = p_top * w0
    g1 = p_top * (ex * w0)

    oh0 = lane == e0
    oh1 = lane == e1
    cnt = jnp.where(jnp.logical_or(oh0, oh1), 1.0, 0.0)
    base = running[...] + _dot(tri[...], cnt.astype(BF16))
    rank0 = jnp.sum(jnp.where(oh0, base, 0.0), axis=1, keepdims=True).astype(jnp.int32)
    rank1 = jnp.sum(jnp.where(oh1, base, 0.0), axis=1, keepdims=True).astype(jnp.int32)
    running[...] = running[...] + jnp.sum(cnt, axis=0, keepdims=True)
    cnt_ref[...] = running[...]

    ri_ref[...] = jnp.where(lane == 0, e0, jnp.where(lane == 1, e1,
                            jnp.where(lane == 2, rank0, jnp.where(lane == 3, rank1, 0))))
    rg_ref[...] = jnp.where(lane == 0, g0, jnp.where(lane == 1, g1, 0.0))


def _router(x, g, wr, br):
    n = x.shape[0]
    const2 = lambda i: (0, 0)
    return pl.pallas_call(
        _router_body,
        grid=(n // RT_TN,),
        in_specs=[
            pl.BlockSpec((RT_TN, D_MODEL), lambda i: (i, 0)),
            pl.BlockSpec((1, D_MODEL), const2),
            pl.BlockSpec((D_MODEL, ROUTER_LANES), const2),
            pl.BlockSpec((1, ROUTER_LANES), const2),
        ],
        out_specs=[
            pl.BlockSpec((RT_TN, D_MODEL), lambda i: (i, 0)),
            pl.BlockSpec((RT_TN, ROUTER_LANES), lambda i: (i, 0)),
            pl.BlockSpec((RT_TN, ROUTER_LANES), lambda i: (i, 0)),
            pl.BlockSpec((1, ROUTER_LANES), const2),
        ],
        out_shape=[
            jax.ShapeDtypeStruct((n, D_MODEL), F32),
            jax.ShapeDtypeStruct((n, ROUTER_LANES), jnp.int32),
            jax.ShapeDtypeStruct((n, ROUTER_LANES), F32),
            jax.ShapeDtypeStruct((1, ROUTER_LANES), F32),
        ],
        scratch_shapes=[pltpu.VMEM((RT_TN, RT_TN), BF16), pltpu.VMEM((1, ROUTER_LANES), F32)],
        compiler_params=_params(("arbitrary",), 32),
        name="router",
    )(x, g, wr, br)


DP_TD = 512
DMA_UNROLL = 8


def _dispatch_body(pad_start, n_pad, n_used, dest_ref, h_hbm, xs_hbm, zblk, sem, pad_sem):
    step = pl.program_id(0)

    def row_copy(tok, dst):
        return pltpu.make_async_copy(h_hbm.at[pl.ds(tok, 1)], xs_hbm.at[pl.ds(dst, 1)], sem)

    def pad_copy(dst):
        return pltpu.make_async_copy(zblk.at[pl.ds(0, 1)], xs_hbm.at[pl.ds(dst, 1)], pad_sem)

    def tail_copy(blk):
        return pltpu.make_async_copy(zblk, xs_hbm.at[pl.ds(blk * ROW_BLOCK, ROW_BLOCK)], pad_sem)

    @pl.when(step == 0)
    def _():
        zblk[...] = jnp.zeros_like(zblk)

        def tail(blk, carry):
            tail_copy(blk).start()
            tail_copy(blk).wait()
            return carry

        lax.fori_loop(n_used[0], N_BLK, tail, 0)

        def per_expert(e, carry):
            def start(j, c):
                pad_copy(pad_start[e] + j).start()
                return c
            lax.fori_loop(0, n_pad[e], start, 0)

            def wait(j, c):
                pad_copy(0).wait()
                return c
            lax.fori_loop(0, n_pad[e], wait, 0)
            return carry

        lax.fori_loop(0, N_EXPERTS, per_expert, 0)

    def start(k, carry):
        row_copy(step * DP_TD + (k >> 1), dest_ref[0, 0, k]).start()
        return carry

    lax.fori_loop(0, TOP_K * DP_TD, start, 0, unroll=DMA_UNROLL)

    def wait(k, carry):
        row_copy(0, 0).wait()
        return carry

    lax.fori_loop(0, TOP_K * DP_TD, wait, 0, unroll=DMA_UNROLL)


def _dispatch(pad_start, n_pad, n_used, dest, h):
    n = h.shape[0]
    dest3 = dest.reshape(n // DP_TD, 1, TOP_K * DP_TD)
    return pl.pallas_call(
        _dispatch_body,
        grid_spec=pltpu.PrefetchScalarGridSpec(
            num_scalar_prefetch=3,
            grid=(n // DP_TD,),
            in_specs=[
                pl.BlockSpec((1, 1, TOP_K * DP_TD), lambda i, ps, npd, nu: (i, 0, 0),
                             memory_space=pltpu.SMEM),
                pl.BlockSpec(memory_space=pl.ANY),
            ],
            out_specs=pl.BlockSpec(memory_space=pl.ANY),
            scratch_shapes=[pltpu.VMEM((ROW_BLOCK, D_MODEL), F32), pltpu.SemaphoreType.DMA,
                            pltpu.SemaphoreType.DMA],
        ),
        out_shape=jax.ShapeDtypeStruct((N_ROWS, D_MODEL), F32),
        compiler_params=_params(("arbitrary",), 16),
        name="moe_dispatch",
    )(pad_start, n_pad, n_used, dest3, h)


def _expert_body(blk_expert, n_used, x_ref, wg_ref, wu_ref, wd_ref, y_ref):
    i = pl.program_id(0)

    @pl.when(i < n_used[0])
    def _():
        x = x_ref[...].astype(BF16)
        a = _dot(x, wg_ref[0].astype(BF16))
        b = _dot(x, wu_ref[0].astype(BF16))
        hmid = (a * (1.0 / (1.0 + jnp.exp(-a)))) * b
        y_ref[...] = _dot(hmid.astype(BF16), wd_ref[0].astype(BF16))

    @pl.when(i >= n_used[0])
    def _():
        y_ref[...] = jnp.zeros_like(y_ref)


def _experts(blk_expert, n_used, xs, w_gate, w_up, w_down):
    def x_map(i, be, nu):
        return (jnp.minimum(i, nu[0] - 1), 0)

    def w_map(i, be, nu):
        return (be[i], 0, 0)

    return pl.pallas_call(
        _expert_body,
        grid_spec=pltpu.PrefetchScalarGridSpec(
            num_scalar_prefetch=2,
            grid=(N_BLK,),
            in_specs=[
                pl.BlockSpec((ROW_BLOCK, D_MODEL), x_map),
                pl.BlockSpec((1, D_MODEL, D_EXPERT), w_map),
                pl.BlockSpec((1, D_MODEL, D_EXPERT), w_map),
                pl.BlockSpec((1, D_EXPERT, D_MODEL), w_map),
            ],
            out_specs=pl.BlockSpec((ROW_BLOCK, D_MODEL), lambda i, be, nu: (i, 0)),
        ),
        out_shape=jax.ShapeDtypeStruct((N_ROWS, D_MODEL), F32),
        compiler_params=_params(("arbitrary",), 40),
        name="moe_experts",
    )(blk_expert, n_used, xs, w_gate, w_up, w_down)


CB_TC = 256


def _combine_body(dest_ref, y_hbm, x_ref, rg_ref, out_ref, ybuf, sem):
    def row_copy(src, slot, t):
        return pltpu.make_async_copy(y_hbm.at[pl.ds(src, 1)], ybuf.at[slot, pl.ds(t, 1)], sem)

    def start(t, carry):
        row_copy(dest_ref[0, 0, 2 * t], 0, t).start()
        row_copy(dest_ref[0, 0, 2 * t + 1], 1, t).start()
        return carry

    lax.fori_loop(0, CB_TC, start, 0, unroll=DMA_UNROLL)

    def wait(t, carry):
        row_copy(0, 0, 0).wait()
        row_copy(0, 1, 0).wait()
        return carry

    lax.fori_loop(0, CB_TC, wait, 0, unroll=DMA_UNROLL)
    rg = rg_ref[...]
    out_ref[...] = x_ref[...] + rg[:, 0:1] * ybuf[0] + rg[:, 1:2] * ybuf[1]


def _combine(dest, y, x, rg):
    n = x.shape[0]
    dest3 = dest.reshape(n // CB_TC, 1, TOP_K * CB_TC)
    return pl.pallas_call(
        _combine_body,
        grid=(n // CB_TC,),
        in_specs=[
            pl.BlockSpec((1, 1, TOP_K * CB_TC), lambda i: (i, 0, 0), memory_space=pltpu.SMEM),
            pl.BlockSpec(memory_space=pl.ANY),
            pl.BlockSpec((CB_TC, D_MODEL), lambda i: (i, 0)),
            pl.BlockSpec((CB_TC, ROUTER_LANES), lambda i: (i, 0)),
        ],
        out_specs=pl.BlockSpec((CB_TC, D_MODEL), lambda i: (i, 0)),
        out_shape=jax.ShapeDtypeStruct((n, D_MODEL), F32),
        scratch_shapes=[pltpu.VMEM((TOP_K, CB_TC, D_MODEL), F32), pltpu.SemaphoreType.DMA],
        compiler_params=_params(("arbitrary",), 16),
        name="moe_combine",
    )(dest3, y, x, rg)


def _moe(x, norm_g, group_w, group_b, exp_w, exp_b, w_gate, w_up, w_down):
    pad_w = ROUTER_LANES - N_GROUPS - N_EXPERTS
    wr = jnp.concatenate([group_w, exp_w, jnp.zeros((D_MODEL, pad_w), F32)], axis=1)
    br = jnp.concatenate([group_b, exp_b, jnp.zeros((pad_w,), F32)])[None, :]
    h, ri, rg, cnt = _router(x, norm_g[None, :], wr, br)

    counts = cnt[0, :N_EXPERTS].astype(jnp.int32)
    pcounts = (counts + ROW_BLOCK - 1) // ROW_BLOCK * ROW_BLOCK
    pends = jnp.cumsum(pcounts)
    pstarts = pends - pcounts
    dest = (pstarts[ri[:, 0:TOP_K]] + ri[:, TOP_K:2 * TOP_K]).astype(jnp.int32)
    blk_expert = jnp.minimum(
        jnp.searchsorted(pends, jnp.arange(N_BLK, dtype=jnp.int32) * ROW_BLOCK, side='right'),
        N_EXPERTS - 1).astype(jnp.int32)
    n_used = (pends[-1:] // ROW_BLOCK).astype(jnp.int32)

    xs = _dispatch((pstarts + counts).astype(jnp.int32), (pcounts - counts).astype(jnp.int32),
                   n_used, dest, h)
    y = _experts(blk_expert, n_used, xs, w_gate, w_up, w_down)
    return _combine(dest, y, x, rg)


FN_TM = 1024


def _final_body(x_ref, g_ref, o_ref):
    o_ref[...] = _rms(x_ref[...], g_ref[...])


def _final_norm(x, g):
    n = x.shape[0]
    return pl.pallas_call(
        _final_body,
        grid=(n // FN_TM,),
        in_specs=[pl.BlockSpec((FN_TM, D_MODEL), lambda i: (i, 0)),
                  pl.BlockSpec((1, D_MODEL), lambda i: (0, 0))],
        out_specs=pl.BlockSpec((FN_TM, D_MODEL), lambda i: (i, 0)),
        out_shape=jax.ShapeDtypeStruct((n, D_MODEL), F32),
        compiler_params=_params(("arbitrary",), 32),
        name="final_norm",
    )(x, g)


def kernel(x, norm_mix, norm_ffn, norm_final, ab_w_in, ab_w_out, pool_w, pool_scale, gm_w_in, gm_norm, gm_ws, gm_bs, gm_w_out, router_group_w, router_group_b, router_expert_w, router_expert_b, moe_w_gate, moe_w_up, moe_w_down):
    assert x.shape == (BATCH, SEQ, D_MODEL) and x.dtype == F32
    xf = x.reshape(N_TOK, D_MODEL)
    for layer in range(DEPTH):
        i = layer // 2
        if layer % 2 == 0:
            zp, qkv = _ab_in(xf, norm_mix[layer][None, :], ab_w_in[i].astype(BF16))
            o = _attention(qkv.reshape(BATCH, SEQ, 3 * ATTN_WIDTH))
            xf = _ab_out(zp.reshape(BATCH, SEQ, POOL_WIDTH), o, xf.reshape(BATCH, SEQ, D_MODEL),
                         pool_w[i].astype(BF16), pool_scale[i][None, :],
                         ab_w_out[i].astype(BF16)).reshape(N_TOK, D_MODEL)
        else:
            bs_bcast = jnp.broadcast_to(gm_bs[i][:, :, None], (GM_GROUPS, GM_CHUNK, GM_CHUNK))
            xf = _gmlp(xf, norm_mix[layer][None, :], gm_w_in[i].astype(BF16), gm_norm[i][None, :],
                       gm_ws[i], bs_bcast, gm_w_out[i].astype(BF16))
        xf = _moe(xf, norm_ffn[layer], router_group_w[layer], router_group_b[layer],
                  router_expert_w[layer], router_expert_b[layer],
                  moe_w_gate[layer], moe_w_up[layer], moe_w_down[layer])
    return _final_norm(xf, norm_final[None, :]).reshape(BATCH, SEQ, D_MODEL)
```

```python
import functools

import jax
import jax.numpy as jnp
from jax import lax
from jax.experimental import pallas as pl
from jax.experimental.pallas import tpu as pltpu

D_MODEL = 1024
BATCH = 8
SEQ = 2048
DEPTH = 4
N_TOK = BATCH * SEQ

POOL_WINDOWS = (2, 4, 8, 16)
POOL_CH = 128
POOL_WIDTH = 512
POOL_HALO = 16
HEAD_DIM = 64
ATTN_WIDTH = 512
HEADS_PER_STEP = 2
ATTN_BLOCK = 128
GM_CHUNK = 128
GM_GROUPS = 8
GM_WIDTH = 1024
N_GROUPS = 8
N_EXPERTS = 64
TOP_K = 2
D_EXPERT = 512
ROW_BLOCK = 128
N_ROWS = (N_TOK * TOP_K + N_EXPERTS * (ROW_BLOCK - 1) + ROW_BLOCK - 1) // ROW_BLOCK * ROW_BLOCK
N_BLK = N_ROWS // ROW_BLOCK
EPS = 1e-6
NEG = -1e30

LANES = 128
V7X_VMEM_BYTES = 64 * 1024 * 1024
MIB = 1024 * 1024

F32 = jnp.float32
BF16 = jnp.bfloat16


def _params(semantics, vmem_mib):
    assert vmem_mib * MIB < V7X_VMEM_BYTES
    return pltpu.CompilerParams(dimension_semantics=semantics, vmem_limit_bytes=vmem_mib * MIB)


def _rms(x, g):
    return x * lax.rsqrt(jnp.mean(x * x, axis=-1, keepdims=True) + EPS) * g


def _dot(a, b):
    return jnp.dot(a, b, preferred_element_type=F32)


def _dot_nt(a, b):
    return lax.dot_general(a, b, (((1,), (1,)), ((), ())), preferred_element_type=F32)


AB_TM = 512


def _ab_in_body(x_ref, g_ref, w_ref, zp_ref, qkv_ref):
    h = _rms(x_ref[...], g_ref[...]).astype(BF16)
    z = _dot(h, w_ref[...])
    zp_ref[...] = z[:, :POOL_WIDTH]
    q = z[:, POOL_WIDTH:POOL_WIDTH + ATTN_WIDTH] * (HEAD_DIM ** -0.5)
    qkv_ref[...] = jnp.concatenate([q, z[:, POOL_WIDTH + ATTN_WIDTH:]], axis=1).astype(BF16)


def _ab_in(x, g, w_bf16):
    n = x.shape[0]
    return pl.pallas_call(
        _ab_in_body,
        grid=(n // AB_TM,),
        in_specs=[
            pl.BlockSpec((AB_TM, D_MODEL), lambda i: (i, 0)),
            pl.BlockSpec((1, D_MODEL), lambda i: (0, 0)),
            pl.BlockSpec((D_MODEL, 4 * ATTN_WIDTH), lambda i: (0, 0)),
        ],
        out_specs=[
            pl.BlockSpec((AB_TM, POOL_WIDTH), lambda i: (i, 0)),
            pl.BlockSpec((AB_TM, 3 * ATTN_WIDTH), lambda i: (i, 0)),
        ],
        out_shape=[
            jax.ShapeDtypeStruct((n, POOL_WIDTH), F32),
            jax.ShapeDtypeStruct((n, 3 * ATTN_WIDTH), BF16),
        ],
        compiler_params=_params(("arbitrary",), 40),
        name="ab_in",
    )(x, g, w_bf16)


ATTN_UNROLL = 4


def _attn_body(q_ref, k_ref, v_ref, o_ref, qf, kf, vf, o1, o4, o16, l1, l4, l16):
    qf[...] = q_ref[...].astype(F32)
    kf[...] = k_ref[...].astype(F32)
    vf[...] = v_ref[...].astype(F32)

    key = lax.broadcasted_iota(jnp.int32, (ATTN_BLOCK, HEADS_PER_STEP * ATTN_BLOCK), 0)
    qry = lax.broadcasted_iota(jnp.int32, (ATTN_BLOCK, HEADS_PER_STEP * ATTN_BLOCK), 1) % ATTN_BLOCK
    cur_ok = key <= qry
    prev_ok = key >= qry
    head0_lane = lax.broadcasted_iota(jnp.int32, (ATTN_BLOCK, LANES), 1) < HEAD_DIM

    def rows(ref, start, d):
        if d == 1:
            return ref[pl.ds(start, ATTN_BLOCK), :]
        return ref[pl.ds(start, ATTN_BLOCK, stride=d), :]

    def put(ref, start, d, val):
        if d == 1:
            ref[pl.ds(start, ATTN_BLOCK), :] = val
        else:
            ref[pl.ds(start, ATTN_BLOCK, stride=d), :] = val

    for d, o_acc, l_acc in ((1, o1, l1), (4, o4, l4), (16, o16, l16)):
        nb = SEQ // d // ATTN_BLOCK
        has_prev = nb > 1

        n_blocks = SEQ // ATTN_BLOCK
        halves = [slice(hh * ATTN_BLOCK, (hh + 1) * ATTN_BLOCK) for hh in range(HEADS_PER_STEP)]
        chans = [slice(hh * HEAD_DIM, (hh + 1) * HEAD_DIM) for hh in range(HEADS_PER_STEP)]

        def block_start(idx, d=d, nb=nb):
            return idx // nb + (idx % nb) * (ATTN_BLOCK * d)

        def scores(idx, k_prev, d=d, has_prev=has_prev):
            start = block_start(idx)
            qv = rows(qf, start, d)
            q2 = jnp.concatenate([jnp.where(head0_lane, qv, 0.0), jnp.where(head0_lane, 0.0, qv)],
                                 axis=0).astype(BF16)
            kc = rows(kf, start, d).astype(BF16)
            keys = jnp.concatenate([k_prev, kc], axis=0) if has_prev else kc
            return _dot_nt(keys, q2), kc

        def finish(pv, den, lse, start, d=d, o_acc=o_acc, l_acc=l_acc):
            out_t = jnp.concatenate([pv[c, h] / den[:, h] for c, h in zip(chans, halves)], axis=0)
            lse_t = jnp.concatenate([jnp.broadcast_to(lse[:, h], (HEAD_DIM, ATTN_BLOCK)) for h in halves],
                                    axis=0)
            put(o_acc, start, d, out_t.T)
            put(l_acc, start, d, lse_t.T)

        def block(idx, carry, d=d, nb=nb, has_prev=has_prev):
            s_raw, k_cur, v_prev_t, pv_last, den_last, lse_last, start_last = carry
            s_next, k_next = scores(jnp.minimum(idx + 1, n_blocks - 1), k_cur)
            finish(pv_last, den_last, lse_last, start_last)
            start = block_start(idx)
            vc_t = rows(vf, start, d).T.astype(BF16)
            if has_prev:
                vals_t = jnp.concatenate([v_prev_t, vc_t], axis=1)
                ok = jnp.concatenate([jnp.logical_and(prev_ok, idx % nb > 0), cur_ok], axis=0)
            else:
                vals_t, ok = vc_t, cur_ok
            s = jnp.where(ok, s_raw, NEG)
            m = jnp.max(s, axis=0, keepdims=True)
            p = jnp.exp(s - m)
            den = jnp.sum(p, axis=0, keepdims=True)
            pv = _dot(vals_t, p.astype(BF16))
            return s_next, k_next, vc_t, pv, den, m + jnp.log(den), start

        k_none = jnp.zeros((ATTN_BLOCK, LANES), BF16)
        s_first, k_first = scores(0, k_none)
        width = HEADS_PER_STEP * ATTN_BLOCK
        init = (s_first, k_first, jnp.zeros((LANES, ATTN_BLOCK), BF16), jnp.zeros((LANES, width), F32),
                jnp.ones((1, width), F32), jnp.zeros((1, width), F32), jnp.int32(0))
        last = lax.fori_loop(0, n_blocks, block, init, unroll=ATTN_UNROLL)
        finish(*last[3:])

    def merge(c, carry):
        sl = pl.ds(pl.multiple_of(c * ATTN_BLOCK, ATTN_BLOCK), ATTN_BLOCK)
        la, lb, lc = l1[sl, :], l4[sl, :], l16[sl, :]
        mx = jnp.maximum(jnp.maximum(la, lb), lc)
        wa, wb, wc = jnp.exp(la - mx), jnp.exp(lb - mx), jnp.exp(lc - mx)
        num = wa * o1[sl, :] + wb * o4[sl, :] + wc * o16[sl, :]
        o_ref[sl, :] = (num / (wa + wb + wc)).astype(o_ref.dtype)
        return carry

    lax.fori_loop(0, SEQ // ATTN_BLOCK, merge, 0)


def _attention(qkv):
    n_hp = ATTN_WIDTH // LANES
    blk = (None, SEQ, LANES)
    return pl.pallas_call(
        _attn_body,
        grid=(qkv.shape[0], n_hp),
        in_specs=[
            pl.BlockSpec(blk, lambda b, hp: (b, 0, hp)),
            pl.BlockSpec(blk, lambda b, hp: (b, 0, n_hp + hp)),
            pl.BlockSpec(blk, lambda b, hp: (b, 0, 2 * n_hp + hp)),
        ],
        out_specs=pl.BlockSpec(blk, lambda b, hp: (b, 0, hp)),
        out_shape=jax.ShapeDtypeStruct((qkv.shape[0], SEQ, ATTN_WIDTH), BF16),
        scratch_shapes=[pltpu.VMEM((SEQ, LANES), F32) for _ in range(9)],
        compiler_params=_params(("arbitrary", "arbitrary"), 32),
        name="dilated_attn",
    )(qkv, qkv, qkv)


AO_TS = 512


def _ab_out_body(zp_ref, halo_ref, o_ref, x_ref, pw_ref, ps_ref, w_ref, out_ref, zz):
    i = pl.program_id(1)
    zz[0:POOL_HALO, :] = jnp.where(i > 0, halo_ref[...], 0.0)
    zz[POOL_HALO:, :] = zp_ref[...]
    pos = i * AO_TS + lax.broadcasted_iota(jnp.int32, (AO_TS, 1), 0)
    parts = []
    for g, w in enumerate(POOL_WINDOWS):
        cols = slice(g * POOL_CH, (g + 1) * POOL_CH)
        acc = zz[pl.ds(POOL_HALO, AO_TS), cols]
        for j in range(1, w):
            acc = acc + zz[pl.ds(POOL_HALO - j, AO_TS), cols]
        count = jnp.minimum(pos + 1, w).astype(F32)
        pooled = acc / count - zp_ref[:, cols]
        parts.append(_dot(pooled.astype(BF16), pw_ref[g]))
    a = jnp.concatenate(parts, axis=1) * ps_ref[...]
    y = jnp.concatenate([a.astype(BF16), o_ref[...]], axis=1)
    out_ref[...] = x_ref[...] + _dot(y, w_ref[...])


def _ab_out(zp, o, x, pool_w_bf16, pool_scale, w_out_bf16):
    halo_per_tile = AO_TS // POOL_HALO
    return pl.pallas_call(
        _ab_out_body,
        grid=(BATCH, SEQ // AO_TS),
        in_specs=[
            pl.BlockSpec((None, AO_TS, POOL_WIDTH), lambda b, i: (b, i, 0)),
            pl.BlockSpec((None, POOL_HALO, POOL_WIDTH),
                         lambda b, i: (b, jnp.maximum(i * halo_per_tile - 1, 0), 0)),
            pl.BlockSpec((None, AO_TS, ATTN_WIDTH), lambda b, i: (b, i, 0)),
            pl.BlockSpec((None, AO_TS, D_MODEL), lambda b, i: (b, i, 0)),
            pl.BlockSpec((len(POOL_WINDOWS), POOL_CH, POOL_CH), lambda b, i: (0, 0, 0)),
            pl.BlockSpec((1, POOL_WIDTH), lambda b, i: (0, 0)),
            pl.BlockSpec((POOL_WIDTH + ATTN_WIDTH, D_MODEL), lambda b, i: (0, 0)),
        ],
        out_specs=pl.BlockSpec((None, AO_TS, D_MODEL), lambda b, i: (b, i, 0)),
        out_shape=jax.ShapeDtypeStruct((BATCH, SEQ, D_MODEL), F32),
        scratch_shapes=[pltpu.VMEM((AO_TS + POOL_HALO, POOL_WIDTH), F32)],
        compiler_params=_params(("arbitrary", "arbitrary"), 32),
        name="pool_ab_out",
    )(zp, zp, o, x, pool_w_bf16, pool_scale, w_out_bf16)


GM_TS = 256


def _gelu_tanh(x):
    return 0.5 * x * (1.0 + jnp.tanh(0.7978845608028654 * (x + 0.044715 * (x * x * x))))


def _gmlp_body(x_ref, g_ref, win_ref, gn_ref, ws_ref, bs_ref, wout_ref, out_ref):
    x = x_ref[...]
    h = _rms(x, g_ref[...]).astype(BF16)
    z = _gelu_tanh(_dot(h, win_ref[...]))
    u = z[:, :GM_WIDTH]
    v = z[:, GM_WIDTH:]
    vc = v - jnp.mean(v, axis=-1, keepdims=True)
    vn = (vc * lax.rsqrt(jnp.mean(vc * vc, axis=-1, keepdims=True) + EPS) * gn_ref[...]).astype(BF16)
    ri = lax.broadcasted_iota(jnp.int32, (GM_CHUNK, GM_CHUNK), 0)
    ci = lax.broadcasted_iota(jnp.int32, (GM_CHUNK, GM_CHUNK), 1)
    causal = ci <= ri
    rows = []
    for c in range(GM_TS // GM_CHUNK):
        rsl = slice(c * GM_CHUNK, (c + 1) * GM_CHUNK)
        cols = []
        for g in range(GM_GROUPS):
            csl = slice(g * (GM_WIDTH // GM_GROUPS), (g + 1) * (GM_WIDTH // GM_GROUPS))
            ws_c = jnp.where(causal, ws_ref[g], 0.0).astype(BF16)
            sv = _dot(ws_c, vn[rsl, csl]) + bs_ref[g]
            cols.append(u[rsl, csl] * sv)
        rows.append(jnp.concatenate(cols, axis=1))
    y = jnp.concatenate(rows, axis=0).astype(BF16)
    out_ref[...] = x + _dot(y, wout_ref[...])


def _gmlp(x, g, w_in_bf16, gm_norm, ws, bs_bcast, w_out_bf16):
    n = x.shape[0]
    const2 = lambda i: (0, 0)
    const3 = lambda i: (0, 0, 0)
    return pl.pallas_call(
        _gmlp_body,
        grid=(n // GM_TS,),
        in_specs=[
            pl.BlockSpec((GM_TS, D_MODEL), lambda i: (i, 0)),
            pl.BlockSpec((1, D_MODEL), const2),
            pl.BlockSpec((D_MODEL, 2 * GM_WIDTH), const2),
            pl.BlockSpec((1, GM_WIDTH), const2),
            pl.BlockSpec((GM_GROUPS, GM_CHUNK, GM_CHUNK), const3),
            pl.BlockSpec((GM_GROUPS, GM_CHUNK, GM_CHUNK), const3),
            pl.BlockSpec((GM_WIDTH, D_MODEL), const2),
        ],
        out_specs=pl.BlockSpec((GM_TS, D_MODEL), lambda i: (i, 0)),
        out_shape=jax.ShapeDtypeStruct((n, D_MODEL), F32),
        compiler_params=_params(("arbitrary",), 40),
        name="gmlp",
    )(x, g, w_in_bf16, gm_norm, ws, bs_bcast, w_out_bf16)


RT_TN = 512
ROUTER_LANES = LANES
EXPERT_LANE0 = N_GROUPS


def _router_body(x_ref, g_ref, wr_ref, br_ref, h_ref, ri_ref, rg_ref, cnt_ref, tri, running):
    step = pl.program_id(0)

    @pl.when(step == 0)
    def _():
        r = lax.broadcasted_iota(jnp.int32, (RT_TN, RT_TN), 0)
        c = lax.broadcasted_iota(jnp.int32, (RT_TN, RT_TN), 1)
        tri[...] = jnp.where(c < r, 1.0, 0.0).astype(BF16)
        running[...] = jnp.zeros_like(running)

    h = _rms(x_ref[...], g_ref[...])
    h_ref[...] = h
    logits = jnp.dot(h, wr_ref[...], preferred_element_type=F32,
                     precision=lax.Precision.HIGHEST) + br_ref[...]
    lane = lax.broadcasted_iota(jnp.int32, (RT_TN, ROUTER_LANES), 1)

    is_group = lane < N_GROUPS
    lg = jnp.where(is_group, logits, NEG)
    mg = jnp.max(lg, axis=1, keepdims=True)
    p_top = 1.0 / jnp.sum(jnp.where(is_group, jnp.exp(logits - mg), 0.0), axis=1, keepdims=True)
    g_idx = jnp.min(jnp.where(lg == mg, lane, ROUTER_LANES), axis=1, keepdims=True)

    in_group = jnp.logical_and(
        jnp.logical_and(lane >= EXPERT_LANE0, lane < EXPERT_LANE0 + N_EXPERTS),
        ((lane - EXPERT_LANE0) >> 3) == g_idx)
    le = jnp.where(in_group, logits, NEG)
    t1 = jnp.max(le, axis=1, keepdims=True)
    i1 = jnp.min(jnp.where(le == t1, lane, ROUTER_LANES), axis=1, keepdims=True)
    le2 = jnp.where(lane == i1, NEG, le)
    t2 = jnp.max(le2, axis=1, keepdims=True)
    i2 = jnp.min(jnp.where(le2 == t2, lane, ROUTER_LANES), axis=1, keepdims=True)
    e0 = i1 - EXPERT_LANE0
    e1 = i2 - EXPERT_LANE0
    ex = jnp.exp(t2 - t1)
    w0 = 1.0 / (1.0 + ex)
    g0 = p_top * w0
    g1 = p_top * (ex * w0)

    oh0 = lane == e0
    oh1 = lane == e1
    cnt = jnp.where(jnp.logical_or(oh0, oh1), 1.0, 0.0)
    base = running[...] + _dot(tri[...], cnt.astype(BF16))
    rank0 = jnp.sum(jnp.where(oh0, base, 0.0), axis=1, keepdims=True).astype(jnp.int32)
    rank1 = jnp.sum(jnp.where(oh1, base, 0.0), axis=1, keepdims=True).astype(jnp.int32)
    running[...] = running[...] + jnp.sum(cnt, axis=0, keepdims=True)
    cnt_ref[...] = running[...]

    ri_ref[...] = jnp.where(lane == 0, e0, jnp.where(lane == 1, e1,
                            jnp.where(lane == 2, rank0, jnp.where(lane == 3, rank1, 0))))
    rg_ref[...] = jnp.where(lane == 0, g0, jnp.where(lane == 1, g1, 0.0))


def _router(x, g, wr, br):
    n = x.shape[0]
    const2 = lambda i: (0, 0)
    return pl.pallas_call(
        _router_body,
        grid=(n // RT_TN,),
        in_specs=[
            pl.BlockSpec((RT_TN, D_MODEL), lambda i: (i, 0)),
            pl.BlockSpec((1, D_MODEL), const2),
            pl.BlockSpec((D_MODEL, ROUTER_LANES), const2),
            pl.BlockSpec((1, ROUTER_LANES), const2),
        ],
        out_specs=[
            pl.BlockSpec((RT_TN, D_MODEL), lambda i: (i, 0)),
            pl.BlockSpec((RT_TN, ROUTER_LANES), lambda i: (i, 0)),
            pl.BlockSpec((RT_TN, ROUTER_LANES), lambda i: (i, 0)),
            pl.BlockSpec((1, ROUTER_LANES), const2),
        ],
        out_shape=[
            jax.ShapeDtypeStruct((n, D_MODEL), F32),
            jax.ShapeDtypeStruct((n, ROUTER_LANES), jnp.int32),
            jax.ShapeDtypeStruct((n, ROUTER_LANES), F32),
            jax.ShapeDtypeStruct((1, ROUTER_LANES), F32),
        ],
        scratch_shapes=[pltpu.VMEM((RT_TN, RT_TN), BF16), pltpu.VMEM((1, ROUTER_LANES), F32)],
        compiler_params=_params(("arbitrary",), 32),
        name="router",
    )(x, g, wr, br)


DP_TD = 256


def _dispatch_body(pad_start, n_pad, n_used, dest_ref, h_ref, xs_hbm, zblk, sem, pad_sem):
    step = pl.program_id(0)

    def row_copy(t, dst):
        return pltpu.make_async_copy(h_ref.at[pl.ds(t, 1)], xs_hbm.at[pl.ds(dst, 1)], sem)

    def pad_copy(dst):
        return pltpu.make_async_copy(zblk.at[pl.ds(0, 1)], xs_hbm.at[pl.ds(dst, 1)], pad_sem)

    def tail_copy(blk):
        return pltpu.make_async_copy(zblk, xs_hbm.at[pl.ds(blk * ROW_BLOCK, ROW_BLOCK)], pad_sem)

    @pl.when(step == 0)
    def _():
        zblk[...] = jnp.zeros_like(zblk)

        def tail(blk, carry):
            tail_copy(blk).start()
            tail_copy(blk).wait()
            return carry

        lax.fori_loop(n_used[0], N_BLK, tail, 0)

        def per_expert(e, carry):
            def start(j, c):
                pad_copy(pad_start[e] + j).start()
                return c
            lax.fori_loop(0, n_pad[e], start, 0)

            def wait(j, c):
                pad_copy(0).wait()
                return c
            lax.fori_loop(0, n_pad[e], wait, 0)
            return carry

        lax.fori_loop(0, N_EXPERTS, per_expert, 0)

    for t in range(DP_TD):
        for s in range(TOP_K):
            row_copy(t, dest_ref[0, 0, TOP_K * t + s]).start()

    def wait(k, carry):
        row_copy(0, 0).wait()
        return carry

    lax.fori_loop(0, TOP_K * DP_TD, wait, 0)


def _dispatch(pad_start, n_pad, n_used, dest, h):
    n = h.shape[0]
    dest3 = dest.reshape(n // DP_TD, 1, TOP_K * DP_TD)
    return pl.pallas_call(
        _dispatch_body,
        grid_spec=pltpu.PrefetchScalarGridSpec(
            num_scalar_prefetch=3,
            grid=(n // DP_TD,),
            in_specs=[
                pl.BlockSpec((1, 1, TOP_K * DP_TD), lambda i, ps, npd, nu: (i, 0, 0),
                             memory_space=pltpu.SMEM),
                pl.BlockSpec((DP_TD, D_MODEL), lambda i, ps, npd, nu: (i, 0)),
            ],
            out_specs=pl.BlockSpec(memory_space=pl.ANY),
            scratch_shapes=[pltpu.VMEM((ROW_BLOCK, D_MODEL), F32), pltpu.SemaphoreType.DMA,
                            pltpu.SemaphoreType.DMA],
        ),
        out_shape=jax.ShapeDtypeStruct((N_ROWS, D_MODEL), F32),
        compiler_params=_params(("arbitrary",), 16),
        name="moe_dispatch",
    )(pad_start, n_pad, n_used, dest3, h)


N_STAGE = 2


def _expert_body(layer, blk_expert, next_expert, n_used, x_ref, wg_hbm, wu_hbm, wd_hbm, y_ref,
                 sg, su, sd, wg, wu, wd, slot_ref, sem):
    i = pl.program_id(0)
    e = blk_expert[i]

    def fetch(expert, slot):
        return (pltpu.make_async_copy(wg_hbm.at[layer, expert], sg.at[slot], sem.at[slot, 0]),
                pltpu.make_async_copy(wu_hbm.at[layer, expert], su.at[slot], sem.at[slot, 1]),
                pltpu.make_async_copy(wd_hbm.at[layer, expert], sd.at[slot], sem.at[slot, 2]))

    @pl.when(i == 0)
    def _():
        slot_ref[0] = 0
        for c in fetch(e, 0):
            c.start()

    used = i < n_used[0]
    first_of_expert = jnp.logical_or(i == 0, blk_expert[jnp.maximum(i - 1, 0)] != e)

    @pl.when(jnp.logical_and(used, first_of_expert))
    def _():
        slot = slot_ref[0]
        nxt = next_expert[e]

        @pl.when(nxt != e)
        def _():
            for c in fetch(nxt, 1 - slot):
                c.start()

        for c in fetch(e, slot):
            c.wait()
        wg[...] = sg[slot].astype(BF16)
        wu[...] = su[slot].astype(BF16)
        wd[...] = sd[slot].astype(BF16)
        slot_ref[0] = 1 - slot

    @pl.when(used)
    def _():
        x = x_ref[...].astype(BF16)
        a = _dot(x, wg[...])
        b = _dot(x, wu[...])
        hmid = (a * (1.0 / (1.0 + jnp.exp(-a)))) * b
        y_ref[...] = _dot(hmid.astype(BF16), wd[...])

    @pl.when(jnp.logical_not(used))
    def _():
        y_ref[...] = jnp.zeros_like(y_ref)


def _experts(layer, blk_expert, next_expert, n_used, xs, w_gate, w_up, w_down):
    def x_map(i, be, ne, nu):
        return (jnp.minimum(i, nu[0] - 1), 0)

    return pl.pallas_call(
        functools.partial(_expert_body, layer),
        grid_spec=pltpu.PrefetchScalarGridSpec(
            num_scalar_prefetch=3,
            grid=(N_BLK,),
            in_specs=[
                pl.BlockSpec((ROW_BLOCK, D_MODEL), x_map),
                pl.BlockSpec(memory_space=pl.ANY),
                pl.BlockSpec(memory_space=pl.ANY),
                pl.BlockSpec(memory_space=pl.ANY),
            ],
            out_specs=pl.BlockSpec((ROW_BLOCK, D_MODEL), lambda i, be, ne, nu: (i, 0)),
            scratch_shapes=[
                pltpu.VMEM((N_STAGE, D_MODEL, D_EXPERT), F32),
                pltpu.VMEM((N_STAGE, D_MODEL, D_EXPERT), F32),
                pltpu.VMEM((N_STAGE, D_EXPERT, D_MODEL), F32),
                pltpu.VMEM((D_MODEL, D_EXPERT), BF16),
                pltpu.VMEM((D_MODEL, D_EXPERT), BF16),
                pltpu.VMEM((D_EXPERT, D_MODEL), BF16),
                pltpu.SMEM((1,), jnp.int32),
                pltpu.SemaphoreType.DMA((N_STAGE, 3)),
            ],
        ),
        out_shape=jax.ShapeDtypeStruct((N_ROWS, D_MODEL), F32),
        compiler_params=_params(("arbitrary",), 32),
        name="moe_experts",
    )(blk_expert, next_expert, n_used, xs, w_gate, w_up, w_down)


CB_TC = 256


def _combine_body(dest_ref, dest_next_ref, y_hbm, x_ref, rg_ref, out_ref, ybuf, sem):
    i = pl.program_id(0)
    n_steps = pl.num_programs(0)
    slot = i % 2

    def row_copy(src, buf, s, t):
        return pltpu.make_async_copy(y_hbm.at[pl.ds(src, 1)], ybuf.at[buf, s, pl.ds(t, 1)], sem.at[buf])

    def issue(idx_ref, buf):
        for t in range(CB_TC):
            for s in range(TOP_K):
                row_copy(idx_ref[0, 0, TOP_K * t + s], buf, s, t).start()

    @pl.when(i == 0)
    def _():
        issue(dest_ref, 0)

    @pl.when(i + 1 < n_steps)
    def _():
        issue(dest_next_ref, 1 - slot)

    def wait(k, carry):
        row_copy(0, slot, 0, 0).wait()
        return carry

    lax.fori_loop(0, TOP_K * CB_TC, wait, 0)
    rg = rg_ref[...]
    out_ref[...] = x_ref[...] + rg[:, 0:1] * ybuf[slot, 0] + rg[:, 1:2] * ybuf[slot, 1]


def _combine(dest, y, x, rg):
    n = x.shape[0]
    n_steps = n // CB_TC
    dest3 = dest.reshape(n_steps, 1, TOP_K * CB_TC)
    return pl.pallas_call(
        _combine_body,
        grid=(n_steps,),
        in_specs=[
            pl.BlockSpec((1, 1, TOP_K * CB_TC), lambda i: (i, 0, 0), memory_space=pltpu.SMEM),
            pl.BlockSpec((1, 1, TOP_K * CB_TC), lambda i: (jnp.minimum(i + 1, n_steps - 1), 0, 0),
                         memory_space=pltpu.SMEM),
            pl.BlockSpec(memory_space=pl.ANY),
            pl.BlockSpec((CB_TC, D_MODEL), lambda i: (i, 0)),
            pl.BlockSpec((CB_TC, ROUTER_LANES), lambda i: (i, 0)),
        ],
        out_specs=pl.BlockSpec((CB_TC, D_MODEL), lambda i: (i, 0)),
        out_shape=jax.ShapeDtypeStruct((n, D_MODEL), F32),
        scratch_shapes=[pltpu.VMEM((2, TOP_K, CB_TC, D_MODEL), F32), pltpu.SemaphoreType.DMA((2,))],
        compiler_params=_params(("arbitrary",), 16),
        name="moe_combine",
    )(dest3, dest3, y, x, rg)


def _moe(layer, x, norm_g, group_w, group_b, exp_w, exp_b, w_gate, w_up, w_down):
    pad_w = ROUTER_LANES - N_GROUPS - N_EXPERTS
    wr = jnp.concatenate([group_w, exp_w, jnp.zeros((D_MODEL, pad_w), F32)], axis=1)
    br = jnp.concatenate([group_b, exp_b, jnp.zeros((pad_w,), F32)])[None, :]
    h, ri, rg, cnt = _router(x, norm_g[None, :], wr, br)

    experts = jnp.arange(N_EXPERTS, dtype=jnp.int32)
    counts = cnt[0, :N_EXPERTS].astype(jnp.int32)
    pcounts = (counts + ROW_BLOCK - 1) // ROW_BLOCK * ROW_BLOCK
    pends = jnp.cumsum(pcounts)
    pstarts = pends - pcounts
    start_of = jnp.sum(jnp.where(ri[:, 0:TOP_K, None] == experts, pstarts, 0), axis=-1)
    dest = (start_of + ri[:, TOP_K:2 * TOP_K]).astype(jnp.int32)
    blk_row0 = jnp.arange(N_BLK, dtype=jnp.int32) * ROW_BLOCK
    blk_expert = jnp.minimum(jnp.sum(pends[None, :] <= blk_row0[:, None], axis=1),
                             N_EXPERTS - 1).astype(jnp.int32)
    n_used = (pends[-1:] // ROW_BLOCK).astype(jnp.int32)
    later_used = jnp.logical_and(experts[None, :] > experts[:, None], (pcounts > 0)[None, :])
    next_expert = jnp.min(jnp.where(later_used, experts[None, :], N_EXPERTS), axis=1)
    next_expert = jnp.where(next_expert == N_EXPERTS, experts, next_expert).astype(jnp.int32)

    xs = _dispatch((pstarts + counts).astype(jnp.int32), (pcounts - counts).astype(jnp.int32),
                   n_used, dest, h)
    y = _experts(layer, blk_expert, next_expert, n_used, xs, w_gate, w_up, w_down)
    return _combine(dest, y, x, rg)


FN_TM = 1024


def _final_body(x_ref, g_ref, o_ref):
    o_ref[...] = _rms(x_ref[...], g_ref[...])


def _final_norm(x, g):
    n = x.shape[0]
    return pl.pallas_call(
        _final_body,
        grid=(n // FN_TM,),
        in_specs=[pl.BlockSpec((FN_TM, D_MODEL), lambda i: (i, 0)),
                  pl.BlockSpec((1, D_MODEL), lambda i: (0, 0))],
        out_specs=pl.BlockSpec((FN_TM, D_MODEL), lambda i: (i, 0)),
        out_shape=jax.ShapeDtypeStruct((n, D_MODEL), F32),
        compiler_params=_params(("arbitrary",), 32),
        name="final_norm",
    )(x, g)


def kernel(x, norm_mix, norm_ffn, norm_final, ab_w_in, ab_w_out, pool_w, pool_scale, gm_w_in, gm_norm, gm_ws, gm_bs, gm_w_out, router_group_w, router_group_b, router_expert_w, router_expert_b, moe_w_gate, moe_w_up, moe_w_down):
    assert x.shape == (BATCH, SEQ, D_MODEL) and x.dtype == F32
    xf = x.reshape(N_TOK, D_MODEL)
    for layer in range(DEPTH):
        i = layer // 2
        if layer % 2 == 0:
            zp, qkv = _ab_in(xf, norm_mix[layer][None, :], ab_w_in[i].astype(BF16))
            o = _attention(qkv.reshape(BATCH, SEQ, 3 * ATTN_WIDTH))
            xf = _ab_out(zp.reshape(BATCH, SEQ, POOL_WIDTH), o, xf.reshape(BATCH, SEQ, D_MODEL),
                         pool_w[i].astype(BF16), pool_scale[i][None, :],
                         ab_w_out[i].astype(BF16)).reshape(N_TOK, D_MODEL)
        else:
            bs_bcast = jnp.broadcast_to(gm_bs[i][:, :, None], (GM_GROUPS, GM_CHUNK, GM_CHUNK))
            xf = _gmlp(xf, norm_mix[layer][None, :], gm_w_in[i].astype(BF16), gm_norm[i][None, :],
                       gm_ws[i], bs_bcast, gm_w_out[i].astype(BF16))
        xf = _moe(layer, xf, norm_ffn[layer], router_group_w[layer], router_group_b[layer],
                  router_expert_w[layer], router_expert_b[layer], moe_w_gate, moe_w_up, moe_w_down)
    return _final_norm(xf, norm_final[None, :]).reshape(BATCH, SEQ, D_MODEL)
```

```python
import functools

import jax
import jax.numpy as jnp
from jax import lax
from jax.experimental import pallas as pl
from jax.experimental.pallas import tpu as pltpu

D_MODEL = 1024
BATCH = 8
SEQ = 2048
DEPTH = 4
N_TOK = BATCH * SEQ

POOL_WINDOWS = (2, 4, 8, 16)
POOL_CH = 128
POOL_WIDTH = 512
POOL_HALO = 16
HEAD_DIM = 64
ATTN_WIDTH = 512
HEADS_PER_STEP = 2
ATTN_BLOCK = 128
GM_CHUNK = 128
GM_GROUPS = 8
GM_WIDTH = 1024
N_GROUPS = 8
N_EXPERTS = 64
TOP_K = 2
D_EXPERT = 512
ROW_BLOCK = 128
N_ROWS = (N_TOK * TOP_K + N_EXPERTS * (ROW_BLOCK - 1) + ROW_BLOCK - 1) // ROW_BLOCK * ROW_BLOCK
N_BLK = N_ROWS // ROW_BLOCK
EPS = 1e-6
NEG = -1e30

LANES = 128
SUBLANES = 8
TOKEN_TILE = (SUBLANES, LANES)
assert SUBLANES * LANES == D_MODEL
V7X_VMEM_BYTES = 64 * 1024 * 1024
MIB = 1024 * 1024

F32 = jnp.float32
BF16 = jnp.bfloat16


def _params(semantics, vmem_mib):
    assert vmem_mib * MIB < V7X_VMEM_BYTES
    return pltpu.CompilerParams(dimension_semantics=semantics, vmem_limit_bytes=vmem_mib * MIB)


def _rms(x, g):
    return x * lax.rsqrt(jnp.mean(x * x, axis=-1, keepdims=True) + EPS) * g


def _dot(a, b):
    return jnp.dot(a, b, preferred_element_type=F32)


def _dot_nt(a, b):
    return lax.dot_general(a, b, (((1,), (1,)), ((), ())), preferred_element_type=F32)


AB_TM = 512


def _ab_in_body(x_ref, g_ref, w_ref, zp_ref, qkv_ref):
    h = _rms(x_ref[...], g_ref[...]).astype(BF16)
    z = _dot(h, w_ref[...])
    zp_ref[...] = z[:, :POOL_WIDTH]
    q = z[:, POOL_WIDTH:POOL_WIDTH + ATTN_WIDTH] * (HEAD_DIM ** -0.5)
    qkv_ref[...] = jnp.concatenate([q, z[:, POOL_WIDTH + ATTN_WIDTH:]], axis=1).astype(BF16)


def _ab_in(x, g, w_bf16):
    n = x.shape[0]
    return pl.pallas_call(
        _ab_in_body,
        grid=(n // AB_TM,),
        in_specs=[
            pl.BlockSpec((AB_TM, D_MODEL), lambda i: (i, 0)),
            pl.BlockSpec((1, D_MODEL), lambda i: (0, 0)),
            pl.BlockSpec((D_MODEL, 4 * ATTN_WIDTH), lambda i: (0, 0)),
        ],
        out_specs=[
            pl.BlockSpec((AB_TM, POOL_WIDTH), lambda i: (i, 0)),
            pl.BlockSpec((AB_TM, 3 * ATTN_WIDTH), lambda i: (i, 0)),
        ],
        out_shape=[
            jax.ShapeDtypeStruct((n, POOL_WIDTH), F32),
            jax.ShapeDtypeStruct((n, 3 * ATTN_WIDTH), BF16),
        ],
        compiler_params=_params(("arbitrary",), 40),
        name="ab_in",
    )(x, g, w_bf16)


ATTN_UNROLL = 4


def _attn_body(q_ref, k_ref, v_ref, o_ref, qf, kf, vf, o1, o4, o16, l1, l4, l16):
    qf[...] = q_ref[...].astype(F32)
    kf[...] = k_ref[...].astype(F32)
    vf[...] = v_ref[...].astype(F32)

    key = lax.broadcasted_iota(jnp.int32, (ATTN_BLOCK, HEADS_PER_STEP * ATTN_BLOCK), 0)
    qry = lax.broadcasted_iota(jnp.int32, (ATTN_BLOCK, HEADS_PER_STEP * ATTN_BLOCK), 1) % ATTN_BLOCK
    cur_ok = key <= qry
    prev_ok = key >= qry
    head0_lane = lax.broadcasted_iota(jnp.int32, (ATTN_BLOCK, LANES), 1) < HEAD_DIM

    def rows(ref, start, d):
        if d == 1:
            return ref[pl.ds(start, ATTN_BLOCK), :]
        return ref[pl.ds(start, ATTN_BLOCK, stride=d), :]

    def put(ref, start, d, val):
        if d == 1:
            ref[pl.ds(start, ATTN_BLOCK), :] = val
        else:
            ref[pl.ds(start, ATTN_BLOCK, stride=d), :] = val

    for d, o_acc, l_acc in ((1, o1, l1), (4, o4, l4), (16, o16, l16)):
        nb = SEQ // d // ATTN_BLOCK
        has_prev = nb > 1

        n_blocks = SEQ // ATTN_BLOCK
        halves = [slice(hh * ATTN_BLOCK, (hh + 1) * ATTN_BLOCK) for hh in range(HEADS_PER_STEP)]
        chans = [slice(hh * HEAD_DIM, (hh + 1) * HEAD_DIM) for hh in range(HEADS_PER_STEP)]

        def block_start(idx, d=d, nb=nb):
            return idx // nb + (idx % nb) * (ATTN_BLOCK * d)

        def scores(idx, k_prev, d=d, has_prev=has_prev):
            start = block_start(idx)
            qv = rows(qf, start, d)
            q2 = jnp.concatenate([jnp.where(head0_lane, qv, 0.0), jnp.where(head0_lane, 0.0, qv)],
                                 axis=0).astype(BF16)
            kc = rows(kf, start, d).astype(BF16)
            keys = jnp.concatenate([k_prev, kc], axis=0) if has_prev else kc
            return _dot_nt(keys, q2), kc

        def finish(pv, den, lse, start, d=d, o_acc=o_acc, l_acc=l_acc):
            out_t = jnp.concatenate([pv[c, h] / den[:, h] for c, h in zip(chans, halves)], axis=0)
            lse_t = jnp.concatenate([jnp.broadcast_to(lse[:, h], (HEAD_DIM, ATTN_BLOCK)) for h in halves],
                                    axis=0)
            put(o_acc, start, d, out_t.T)
            put(l_acc, start, d, lse_t.T)

        def block(idx, carry, d=d, nb=nb, has_prev=has_prev):
            s_raw, k_cur, v_prev_t, pv_last, den_last, lse_last, start_last = carry
            s_next, k_next = scores(jnp.minimum(idx + 1, n_blocks - 1), k_cur)
            finish(pv_last, den_last, lse_last, start_last)
            start = block_start(idx)
            vc_t = rows(vf, start, d).T.astype(BF16)
            if has_prev:
                vals_t = jnp.concatenate([v_prev_t, vc_t], axis=1)
                ok = jnp.concatenate([jnp.logical_and(prev_ok, idx % nb > 0), cur_ok], axis=0)
            else:
                vals_t, ok = vc_t, cur_ok
            s = jnp.where(ok, s_raw, NEG)
            m = jnp.max(s, axis=0, keepdims=True)
            p = jnp.exp(s - m)
            den = jnp.sum(p, axis=0, keepdims=True)
            pv = _dot(vals_t, p.astype(BF16))
            return s_next, k_next, vc_t, pv, den, m + jnp.log(den), start

        k_none = jnp.zeros((ATTN_BLOCK, LANES), BF16)
        s_first, k_first = scores(0, k_none)
        width = HEADS_PER_STEP * ATTN_BLOCK
        init = (s_first, k_first, jnp.zeros((LANES, ATTN_BLOCK), BF16), jnp.zeros((LANES, width), F32),
                jnp.ones((1, width), F32), jnp.zeros((1, width), F32), jnp.int32(0))
        last = lax.fori_loop(0, n_blocks, block, init, unroll=ATTN_UNROLL)
        finish(*last[3:])

    def merge(c, carry):
        sl = pl.ds(pl.multiple_of(c * ATTN_BLOCK, ATTN_BLOCK), ATTN_BLOCK)
        la, lb, lc = l1[sl, :], l4[sl, :], l16[sl, :]
        mx = jnp.maximum(jnp.maximum(la, lb), lc)
        wa, wb, wc = jnp.exp(la - mx), jnp.exp(lb - mx), jnp.exp(lc - mx)
        num = wa * o1[sl, :] + wb * o4[sl, :] + wc * o16[sl, :]
        o_ref[sl, :] = (num / (wa + wb + wc)).astype(o_ref.dtype)
        return carry

    lax.fori_loop(0, SEQ // ATTN_BLOCK, merge, 0)


def _attention(qkv):
    n_hp = ATTN_WIDTH // LANES
    blk = (None, SEQ, LANES)
    return pl.pallas_call(
        _attn_body,
        grid=(qkv.shape[0], n_hp),
        in_specs=[
            pl.BlockSpec(blk, lambda b, hp: (b, 0, hp)),
            pl.BlockSpec(blk, lambda b, hp: (b, 0, n_hp + hp)),
            pl.BlockSpec(blk, lambda b, hp: (b, 0, 2 * n_hp + hp)),
        ],
        out_specs=pl.BlockSpec(blk, lambda b, hp: (b, 0, hp)),
        out_shape=jax.ShapeDtypeStruct((qkv.shape[0], SEQ, ATTN_WIDTH), BF16),
        scratch_shapes=[pltpu.VMEM((SEQ, LANES), F32) for _ in range(9)],
        compiler_params=_params(("arbitrary", "arbitrary"), 32),
        name="dilated_attn",
    )(qkv, qkv, qkv)


AO_TS = 512


def _ab_out_body(zp_ref, halo_ref, o_ref, x_ref, pw_ref, ps_ref, w_ref, out_ref, zz):
    i = pl.program_id(1)
    zz[0:POOL_HALO, :] = jnp.where(i > 0, halo_ref[...], 0.0)
    zz[POOL_HALO:, :] = zp_ref[...]
    pos = i * AO_TS + lax.broadcasted_iota(jnp.int32, (AO_TS, 1), 0)
    parts = []
    for g, w in enumerate(POOL_WINDOWS):
        cols = slice(g * POOL_CH, (g + 1) * POOL_CH)
        acc = zz[pl.ds(POOL_HALO, AO_TS), cols]
        for j in range(1, w):
            acc = acc + zz[pl.ds(POOL_HALO - j, AO_TS), cols]
        count = jnp.minimum(pos + 1, w).astype(F32)
        pooled = acc / count - zp_ref[:, cols]
        parts.append(_dot(pooled.astype(BF16), pw_ref[g]))
    a = jnp.concatenate(parts, axis=1) * ps_ref[...]
    y = jnp.concatenate([a.astype(BF16), o_ref[...]], axis=1)
    out_ref[...] = x_ref[...] + _dot(y, w_ref[...])


def _ab_out(zp, o, x, pool_w_bf16, pool_scale, w_out_bf16):
    halo_per_tile = AO_TS // POOL_HALO
    return pl.pallas_call(
        _ab_out_body,
        grid=(BATCH, SEQ // AO_TS),
        in_specs=[
            pl.BlockSpec((None, AO_TS, POOL_WIDTH), lambda b, i: (b, i, 0)),
            pl.BlockSpec((None, POOL_HALO, POOL_WIDTH),
                         lambda b, i: (b, jnp.maximum(i * halo_per_tile - 1, 0), 0)),
            pl.BlockSpec((None, AO_TS, ATTN_WIDTH), lambda b, i: (b, i, 0)),
            pl.BlockSpec((None, AO_TS, D_MODEL), lambda b, i: (b, i, 0)),
            pl.BlockSpec((len(POOL_WINDOWS), POOL_CH, POOL_CH), lambda b, i: (0, 0, 0)),
            pl.BlockSpec((1, POOL_WIDTH), lambda b, i: (0, 0)),
            pl.BlockSpec((POOL_WIDTH + ATTN_WIDTH, D_MODEL), lambda b, i: (0, 0)),
        ],
        out_specs=pl.BlockSpec((None, AO_TS, D_MODEL), lambda b, i: (b, i, 0)),
        out_shape=jax.ShapeDtypeStruct((BATCH, SEQ, D_MODEL), F32),
        scratch_shapes=[pltpu.VMEM((AO_TS + POOL_HALO, POOL_WIDTH), F32)],
        compiler_params=_params(("arbitrary", "arbitrary"), 32),
        name="pool_ab_out",
    )(zp, zp, o, x, pool_w_bf16, pool_scale, w_out_bf16)


GM_TS = 256


def _gelu_tanh(x):
    return 0.5 * x * (1.0 + jnp.tanh(0.7978845608028654 * (x + 0.044715 * (x * x * x))))


def _gmlp_body(x_ref, g_ref, win_ref, gn_ref, ws_ref, bs_ref, wout_ref, out_ref):
    x = x_ref[...]
    h = _rms(x, g_ref[...]).astype(BF16)
    z = _gelu_tanh(_dot(h, win_ref[...]))
    u = z[:, :GM_WIDTH]
    v = z[:, GM_WIDTH:]
    vc = v - jnp.mean(v, axis=-1, keepdims=True)
    vn = (vc * lax.rsqrt(jnp.mean(vc * vc, axis=-1, keepdims=True) + EPS) * gn_ref[...]).astype(BF16)
    ri = lax.broadcasted_iota(jnp.int32, (GM_CHUNK, GM_CHUNK), 0)
    ci = lax.broadcasted_iota(jnp.int32, (GM_CHUNK, GM_CHUNK), 1)
    causal = ci <= ri
    rows = []
    for c in range(GM_TS // GM_CHUNK):
        rsl = slice(c * GM_CHUNK, (c + 1) * GM_CHUNK)
        cols = []
        for g in range(GM_GROUPS):
            csl = slice(g * (GM_WIDTH // GM_GROUPS), (g + 1) * (GM_WIDTH // GM_GROUPS))
            ws_c = jnp.where(causal, ws_ref[g], 0.0).astype(BF16)
            sv = _dot(ws_c, vn[rsl, csl]) + bs_ref[g]
            cols.append(u[rsl, csl] * sv)
        rows.append(jnp.concatenate(cols, axis=1))
    y = jnp.concatenate(rows, axis=0).astype(BF16)
    out_ref[...] = x + _dot(y, wout_ref[...])


def _gmlp(x, g, w_in_bf16, gm_norm, ws, bs_bcast, w_out_bf16):
    n = x.shape[0]
    const2 = lambda i: (0, 0)
    const3 = lambda i: (0, 0, 0)
    return pl.pallas_call(
        _gmlp_body,
        grid=(n // GM_TS,),
        in_specs=[
            pl.BlockSpec((GM_TS, D_MODEL), lambda i: (i, 0)),
            pl.BlockSpec((1, D_MODEL), const2),
            pl.BlockSpec((D_MODEL, 2 * GM_WIDTH), const2),
            pl.BlockSpec((1, GM_WIDTH), const2),
            pl.BlockSpec((GM_GROUPS, GM_CHUNK, GM_CHUNK), const3),
            pl.BlockSpec((GM_GROUPS, GM_CHUNK, GM_CHUNK), const3),
            pl.BlockSpec((GM_WIDTH, D_MODEL), const2),
        ],
        out_specs=pl.BlockSpec((GM_TS, D_MODEL), lambda i: (i, 0)),
        out_shape=jax.ShapeDtypeStruct((n, D_MODEL), F32),
        compiler_params=_params(("arbitrary",), 40),
        name="gmlp",
    )(x, g, w_in_bf16, gm_norm, ws, bs_bcast, w_out_bf16)


RT_TN = 512
ROUTER_LANES = LANES
EXPERT_LANE0 = N_GROUPS


def _router_body(x_ref, g_ref, wr_ref, br_ref, h_ref, ri_ref, rg_ref, cnt_ref, tri, running):
    step = pl.program_id(0)

    @pl.when(step == 0)
    def _():
        r = lax.broadcasted_iota(jnp.int32, (RT_TN, RT_TN), 0)
        c = lax.broadcasted_iota(jnp.int32, (RT_TN, RT_TN), 1)
        tri[...] = jnp.where(c < r, 1.0, 0.0).astype(BF16)
        running[...] = jnp.zeros_like(running)

    h = _rms(x_ref[...], g_ref[...])
    h_ref[...] = h.reshape(RT_TN, *TOKEN_TILE)
    logits = jnp.dot(h, wr_ref[...], preferred_element_type=F32,
                     precision=lax.Precision.HIGHEST) + br_ref[...]
    lane = lax.broadcasted_iota(jnp.int32, (RT_TN, ROUTER_LANES), 1)

    is_group = lane < N_GROUPS
    lg = jnp.where(is_group, logits, NEG)
    mg = jnp.max(lg, axis=1, keepdims=True)
    p_top = 1.0 / jnp.sum(jnp.where(is_group, jnp.exp(logits - mg), 0.0), axis=1, keepdims=True)
    g_idx = jnp.min(jnp.where(lg == mg, lane, ROUTER_LANES), axis=1, keepdims=True)

    in_group = jnp.logical_and(
        jnp.logical_and(lane >= EXPERT_LANE0, lane < EXPERT_LANE0 + N_EXPERTS),
        ((lane - EXPERT_LANE0) >> 3) == g_idx)
    le = jnp.where(in_group, logits, NEG)
    t1 = jnp.max(le, axis=1, keepdims=True)
    i1 = jnp.min(jnp.where(le == t1, lane, ROUTER_LANES), axis=1, keepdims=True)
    le2 = jnp.where(lane == i1, NEG, le)
    t2 = jnp.max(le2, axis=1, keepdims=True)
    i2 = jnp.min(jnp.where(le2 == t2, lane, ROUTER_LANES), axis=1, keepdims=True)
    e0 = i1 - EXPERT_LANE0
    e1 = i2 - EXPERT_LANE0
    ex = jnp.exp(t2 - t1)
    w0 = 1.0 / (1.0 + ex)
    g0 = p_top * w0
    g1 = p_top * (ex * w0)

    oh0 = lane == e0
    oh1 = lane == e1
    cnt = jnp.where(jnp.logical_or(oh0, oh1), 1.0, 0.0)
    base = running[...] + _dot(tri[...], cnt.astype(BF16))
    rank0 = jnp.sum(jnp.where(oh0, base, 0.0), axis=1, keepdims=True).astype(jnp.int32)
    rank1 = jnp.sum(jnp.where(oh1, base, 0.0), axis=1, keepdims=True).astype(jnp.int32)
    running[...] = running[...] + jnp.sum(cnt, axis=0, keepdims=True)
    cnt_ref[...] = running[...]

    ri_ref[...] = jnp.where(lane == 0, e0, jnp.where(lane == 1, e1,
                            jnp.where(lane == 2, rank0, jnp.where(lane == 3, rank1, 0))))
    rg_ref[...] = jnp.where(lane == 0, g0, jnp.where(lane == 1, g1, 0.0))


def _router(x, g, wr, br):
    n = x.shape[0]
    const2 = lambda i: (0, 0)
    return pl.pallas_call(
        _router_body,
        grid=(n // RT_TN,),
        in_specs=[
            pl.BlockSpec((RT_TN, D_MODEL), lambda i: (i, 0)),
            pl.BlockSpec((1, D_MODEL), const2),
            pl.BlockSpec((D_MODEL, ROUTER_LANES), const2),
            pl.BlockSpec((1, ROUTER_LANES), const2),
        ],
        out_specs=[
            pl.BlockSpec((RT_TN, *TOKEN_TILE), lambda i: (i, 0, 0)),
            pl.BlockSpec((RT_TN, ROUTER_LANES), lambda i: (i, 0)),
            pl.BlockSpec((RT_TN, ROUTER_LANES), lambda i: (i, 0)),
            pl.BlockSpec((1, ROUTER_LANES), const2),
        ],
        out_shape=[
            jax.ShapeDtypeStruct((n, *TOKEN_TILE), F32),
            jax.ShapeDtypeStruct((n, ROUTER_LANES), jnp.int32),
            jax.ShapeDtypeStruct((n, ROUTER_LANES), F32),
            jax.ShapeDtypeStruct((1, ROUTER_LANES), F32),
        ],
        scratch_shapes=[pltpu.VMEM((RT_TN, RT_TN), BF16), pltpu.VMEM((1, ROUTER_LANES), F32)],
        compiler_params=_params(("arbitrary",), 32),
        name="router",
    )(x, g, wr, br)


DP_TD = 256


def _dispatch_body(pad_start, n_pad, n_used, dest_ref, h_ref, xs_hbm, zblk, sem, pad_sem):
    step = pl.program_id(0)

    def row_copy(t, dst):
        return pltpu.make_async_copy(h_ref.at[t], xs_hbm.at[dst], sem)

    def pad_copy(dst):
        return pltpu.make_async_copy(zblk.at[0], xs_hbm.at[dst], pad_sem)

    def tail_copy(blk):
        return pltpu.make_async_copy(zblk, xs_hbm.at[pl.ds(blk * ROW_BLOCK, ROW_BLOCK)], pad_sem)

    @pl.when(step == 0)
    def _():
        zblk[...] = jnp.zeros_like(zblk)

        def tail(blk, carry):
            tail_copy(blk).start()
            tail_copy(blk).wait()
            return carry

        lax.fori_loop(n_used[0], N_BLK, tail, 0)

        def per_expert(e, carry):
            def start(j, c):
                pad_copy(pad_start[e] + j).start()
                return c
            lax.fori_loop(0, n_pad[e], start, 0)

            def wait(j, c):
                pad_copy(0).wait()
                return c
            lax.fori_loop(0, n_pad[e], wait, 0)
            return carry

        lax.fori_loop(0, N_EXPERTS, per_expert, 0)

    for t in range(DP_TD):
        for s in range(TOP_K):
            row_copy(t, dest_ref[0, 0, TOP_K * t + s]).start()

    def wait(k, carry):
        row_copy(0, 0).wait()
        return carry

    lax.fori_loop(0, TOP_K * DP_TD, wait, 0)


def _dispatch(pad_start, n_pad, n_used, dest, h):
    n = h.shape[0]
    dest3 = dest.reshape(n // DP_TD, 1, TOP_K * DP_TD)
    return pl.pallas_call(
        _dispatch_body,
        grid_spec=pltpu.PrefetchScalarGridSpec(
            num_scalar_prefetch=3,
            grid=(n // DP_TD,),
            in_specs=[
                pl.BlockSpec((1, 1, TOP_K * DP_TD), lambda i, ps, npd, nu: (i, 0, 0),
                             memory_space=pltpu.SMEM),
                pl.BlockSpec((DP_TD, *TOKEN_TILE), lambda i, ps, npd, nu: (i, 0, 0)),
            ],
            out_specs=pl.BlockSpec(memory_space=pl.ANY),
            scratch_shapes=[pltpu.VMEM((ROW_BLOCK, *TOKEN_TILE), F32), pltpu.SemaphoreType.DMA,
                            pltpu.SemaphoreType.DMA],
        ),
        out_shape=jax.ShapeDtypeStruct((N_ROWS, *TOKEN_TILE), F32),
        compiler_params=_params(("arbitrary",), 16),
        name="moe_dispatch",
    )(pad_start, n_pad, n_used, dest3, h)


N_STAGE = 2
WEIGHT_DMA_PRIORITY = 1


def _expert_body(layer, blk_expert, next_expert, n_used, x_ref, wg_hbm, wu_hbm, wd_hbm, y_ref,
                 sg, su, sd, wg, wu, wd, slot_ref, sem):
    i = pl.program_id(0)
    e = blk_expert[i]

    def fetch(expert, slot):
        return (pltpu.make_async_copy(wg_hbm.at[layer, expert], sg.at[slot], sem.at[slot, 0]),
                pltpu.make_async_copy(wu_hbm.at[layer, expert], su.at[slot], sem.at[slot, 1]),
                pltpu.make_async_copy(wd_hbm.at[layer, expert], sd.at[slot], sem.at[slot, 2]))

    def start_fetch(expert, slot):
        for c in fetch(expert, slot):
            c.start(priority=WEIGHT_DMA_PRIORITY)

    @pl.when(i == 0)
    def _():
        slot_ref[0] = 0
        start_fetch(e, 0)

    used = i < n_used[0]
    first_of_expert = jnp.logical_or(i == 0, blk_expert[jnp.maximum(i - 1, 0)] != e)

    @pl.when(jnp.logical_and(used, first_of_expert))
    def _():
        slot = slot_ref[0]
        nxt = next_expert[e]

        @pl.when(nxt != e)
        def _():
            start_fetch(nxt, 1 - slot)

        for c in fetch(e, slot):
            c.wait()
        wg[...] = sg[slot].astype(BF16)
        wu[...] = su[slot].astype(BF16)
        wd[...] = sd[slot].astype(BF16)
        slot_ref[0] = 1 - slot

    @pl.when(used)
    def _():
        x = x_ref[...].reshape(ROW_BLOCK, D_MODEL).astype(BF16)
        a = _dot(x, wg[...])
        b = _dot(x, wu[...])
        hmid = (a * (1.0 / (1.0 + jnp.exp(-a)))) * b
        y_ref[...] = _dot(hmid.astype(BF16), wd[...]).reshape(ROW_BLOCK, *TOKEN_TILE)

    @pl.when(jnp.logical_not(used))
    def _():
        y_ref[...] = jnp.zeros_like(y_ref)


def _experts(layer, blk_expert, next_expert, n_used, xs, w_gate, w_up, w_down):
    def x_map(i, be, ne, nu):
        return (jnp.maximum(jnp.minimum(i, nu[0] - 1), 0), 0, 0)

    return pl.pallas_call(
        functools.partial(_expert_body, layer),
        grid_spec=pltpu.PrefetchScalarGridSpec(
            num_scalar_prefetch=3,
            grid=(N_BLK,),
            in_specs=[
                pl.BlockSpec((ROW_BLOCK, *TOKEN_TILE), x_map),
                pl.BlockSpec(memory_space=pl.ANY),
                pl.BlockSpec(memory_space=pl.ANY),
                pl.BlockSpec(memory_space=pl.ANY),
            ],
            out_specs=pl.BlockSpec((ROW_BLOCK, *TOKEN_TILE), lambda i, be, ne, nu: (i, 0, 0)),
            scratch_shapes=[
                pltpu.VMEM((N_STAGE, D_MODEL, D_EXPERT), F32),
                pltpu.VMEM((N_STAGE, D_MODEL, D_EXPERT), F32),
                pltpu.VMEM((N_STAGE, D_EXPERT, D_MODEL), F32),
                pltpu.VMEM((D_MODEL, D_EXPERT), BF16),
                pltpu.VMEM((D_MODEL, D_EXPERT), BF16),
                pltpu.VMEM((D_EXPERT, D_MODEL), BF16),
                pltpu.SMEM((1,), jnp.int32),
                pltpu.SemaphoreType.DMA((N_STAGE, 3)),
            ],
        ),
        out_shape=jax.ShapeDtypeStruct((N_ROWS, *TOKEN_TILE), F32),
        compiler_params=_params(("arbitrary",), 32),
        name="moe_experts",
    )(blk_expert, next_expert, n_used, xs, w_gate, w_up, w_down)


CB_TC = 256


def _combine_body(final, dest_ref, dest_next_ref, y_hbm, x_ref, rg_ref, fg_ref, out_ref, ybuf, sem):
    i = pl.program_id(0)
    n_steps = pl.num_programs(0)
    slot = i % 2

    def row_copy(src, buf, s, t):
        return pltpu.make_async_copy(y_hbm.at[src], ybuf.at[buf, s, t], sem.at[buf])

    def issue(idx_ref, buf):
        for t in range(CB_TC):
            for s in range(TOP_K):
                row_copy(idx_ref[0, 0, TOP_K * t + s], buf, s, t).start()

    @pl.when(i == 0)
    def _():
        issue(dest_ref, 0)

    @pl.when(i + 1 < n_steps)
    def _():
        issue(dest_next_ref, 1 - slot)

    def wait(k, carry):
        row_copy(0, slot, 0, 0).wait()
        return carry

    lax.fori_loop(0, TOP_K * CB_TC, wait, 0)
    rg = rg_ref[...]
    y0 = ybuf[slot, 0].reshape(CB_TC, D_MODEL)
    y1 = ybuf[slot, 1].reshape(CB_TC, D_MODEL)
    out = x_ref[...] + rg[:, 0:1] * y0 + rg[:, 1:2] * y1
    out_ref[...] = _rms(out, fg_ref[...]) if final else out


def _combine(dest, y, x, rg, final_g, final):
    n = x.shape[0]
    n_steps = n // CB_TC
    dest3 = dest.reshape(n_steps, 1, TOP_K * CB_TC)
    return pl.pallas_call(
        functools.partial(_combine_body, final),
        grid=(n_steps,),
        in_specs=[
            pl.BlockSpec((1, 1, TOP_K * CB_TC), lambda i: (i, 0, 0), memory_space=pltpu.SMEM),
            pl.BlockSpec((1, 1, TOP_K * CB_TC), lambda i: (jnp.minimum(i + 1, n_steps - 1), 0, 0),
                         memory_space=pltpu.SMEM),
            pl.BlockSpec(memory_space=pl.ANY),
            pl.BlockSpec((CB_TC, D_MODEL), lambda i: (i, 0)),
            pl.BlockSpec((CB_TC, ROUTER_LANES), lambda i: (i, 0)),
            pl.BlockSpec((1, D_MODEL), lambda i: (0, 0)),
        ],
        out_specs=pl.BlockSpec((CB_TC, D_MODEL), lambda i: (i, 0)),
        out_shape=jax.ShapeDtypeStruct((n, D_MODEL), F32),
        scratch_shapes=[pltpu.VMEM((2, TOP_K, CB_TC, *TOKEN_TILE), F32), pltpu.SemaphoreType.DMA((2,))],
        compiler_params=_params(("arbitrary",), 16),
        name="moe_combine",
    )(dest3, dest3, y, x, rg, final_g)


def _moe(layer, x, norm_g, group_w, group_b, exp_w, exp_b, w_gate, w_up, w_down, final_g, final):
    pad_w = ROUTER_LANES - N_GROUPS - N_EXPERTS
    wr = jnp.concatenate([group_w, exp_w, jnp.zeros((D_MODEL, pad_w), F32)], axis=1)
    br = jnp.concatenate([group_b, exp_b, jnp.zeros((pad_w,), F32)])[None, :]
    h, ri, rg, cnt = _router(x, norm_g[None, :], wr, br)

    experts = jnp.arange(N_EXPERTS, dtype=jnp.int32)
    counts = cnt[0, :N_EXPERTS].astype(jnp.int32)
    pcounts = (counts + ROW_BLOCK - 1) // ROW_BLOCK * ROW_BLOCK
    pends = jnp.cumsum(pcounts)
    pstarts = pends - pcounts
    start_of = jnp.sum(jnp.where(ri[:, 0:TOP_K, None] == experts, pstarts, 0), axis=-1)
    dest = (start_of + ri[:, TOP_K:2 * TOP_K]).astype(jnp.int32)
    blk_row0 = jnp.arange(N_BLK, dtype=jnp.int32) * ROW_BLOCK
    blk_expert = jnp.minimum(jnp.sum(pends[None, :] <= blk_row0[:, None], axis=1),
                             N_EXPERTS - 1).astype(jnp.int32)
    n_used = (pends[-1:] // ROW_BLOCK).astype(jnp.int32)
    later_used = jnp.logical_and(experts[None, :] > experts[:, None], (pcounts > 0)[None, :])
    next_expert = jnp.min(jnp.where(later_used, experts[None, :], N_EXPERTS), axis=1)
    next_expert = jnp.where(next_expert == N_EXPERTS, experts, next_expert).astype(jnp.int32)

    xs = _dispatch((pstarts + counts).astype(jnp.int32), (pcounts - counts).astype(jnp.int32),
                   n_used, dest, h)
    y = _experts(layer, blk_expert, next_expert, n_used, xs, w_gate, w_up, w_down)
    return _combine(dest, y, x, rg, final_g, final)


def kernel(x, norm_mix, norm_ffn, norm_final, ab_w_in, ab_w_out, pool_w, pool_scale, gm_w_in, gm_norm, gm_ws, gm_bs, gm_w_out, router_group_w, router_group_b, router_expert_w, router_expert_b, moe_w_gate, moe_w_up, moe_w_down):
    assert x.shape == (BATCH, SEQ, D_MODEL) and x.dtype == F32
    xf = x.reshape(N_TOK, D_MODEL)
    for layer in range(DEPTH):
        i = layer // 2
        if layer % 2 == 0:
            zp, qkv = _ab_in(xf, norm_mix[layer][None, :], ab_w_in[i].astype(BF16))
            o = _attention(qkv.reshape(BATCH, SEQ, 3 * ATTN_WIDTH))
            xf = _ab_out(zp.reshape(BATCH, SEQ, POOL_WIDTH), o, xf.reshape(BATCH, SEQ, D_MODEL),
                         pool_w[i].astype(BF16), pool_scale[i][None, :],
                         ab_w_out[i].astype(BF16)).reshape(N_TOK, D_MODEL)
        else:
            bs_bcast = jnp.broadcast_to(gm_bs[i][:, :, None], (GM_GROUPS, GM_CHUNK, GM_CHUNK))
            xf = _gmlp(xf, norm_mix[layer][None, :], gm_w_in[i].astype(BF16), gm_norm[i][None, :],
                       gm_ws[i], bs_bcast, gm_w_out[i].astype(BF16))
        xf = _moe(layer, xf, norm_ffn[layer], router_group_w[layer], router_group_b[layer],
                  router_expert_w[layer], router_expert_b[layer], moe_w_gate, moe_w_up, moe_w_down,
                  norm_final[None, :], layer == DEPTH - 1)
    return xf.reshape(BATCH, SEQ, D_MODEL)
```

```python
import functools

import jax
import jax.numpy as jnp
from jax import lax
from jax.experimental import pallas as pl
from jax.experimental.pallas import tpu as pltpu

D_MODEL = 1024
BATCH = 8
SEQ = 2048
DEPTH = 4
N_TOK = BATCH * SEQ

POOL_WINDOWS = (2, 4, 8, 16)
POOL_CH = 128
POOL_WIDTH = 512
POOL_HALO = 16
HEAD_DIM = 64
ATTN_WIDTH = 512
HEADS_PER_STEP = 2
ATTN_BLOCK = 128
GM_CHUNK = 128
GM_GROUPS = 8
GM_WIDTH = 1024
N_GROUPS = 8
N_EXPERTS = 64
TOP_K = 2
D_EXPERT = 512
ROW_BLOCK = 128
N_ROWS = (N_TOK * TOP_K + N_EXPERTS * (ROW_BLOCK - 1) + ROW_BLOCK - 1) // ROW_BLOCK * ROW_BLOCK
N_BLK = N_ROWS // ROW_BLOCK
EPS = 1e-6
NEG = -1e30

LANES = 128
SUBLANES = 8
TOKEN_TILE = (SUBLANES, LANES)
assert SUBLANES * LANES == D_MODEL
V7X_VMEM_BYTES = 64 * 1024 * 1024
MIB = 1024 * 1024

F32 = jnp.float32
BF16 = jnp.bfloat16


def _params(semantics, vmem_mib):
    assert vmem_mib * MIB < V7X_VMEM_BYTES
    return pltpu.CompilerParams(dimension_semantics=semantics, vmem_limit_bytes=vmem_mib * MIB)


def _rms(x, g):
    return x * lax.rsqrt(jnp.mean(x * x, axis=-1, keepdims=True) + EPS) * g


def _dot(a, b):
    return jnp.dot(a, b, preferred_element_type=F32)


def _dot_nt(a, b):
    return lax.dot_general(a, b, (((1,), (1,)), ((), ())), preferred_element_type=F32)


AB_TM = 512


def _ab_in_body(x_ref, g_ref, w_ref, zp_ref, qkv_ref):
    h = _rms(x_ref[...], g_ref[...]).astype(BF16)
    z = _dot(h, w_ref[...])
    zp_ref[...] = z[:, :POOL_WIDTH]
    q = z[:, POOL_WIDTH:POOL_WIDTH + ATTN_WIDTH] * (HEAD_DIM ** -0.5)
    qkv_ref[...] = jnp.concatenate([q, z[:, POOL_WIDTH + ATTN_WIDTH:]], axis=1).astype(BF16)


def _ab_in(x, g, w_bf16):
    n = x.shape[0]
    return pl.pallas_call(
        _ab_in_body,
        grid=(n // AB_TM,),
        in_specs=[
            pl.BlockSpec((AB_TM, D_MODEL), lambda i: (i, 0)),
            pl.BlockSpec((1, D_MODEL), lambda i: (0, 0)),
            pl.BlockSpec((D_MODEL, 4 * ATTN_WIDTH), lambda i: (0, 0)),
        ],
        out_specs=[
            pl.BlockSpec((AB_TM, POOL_WIDTH), lambda i: (i, 0)),
            pl.BlockSpec((AB_TM, 3 * ATTN_WIDTH), lambda i: (i, 0)),
        ],
        out_shape=[
            jax.ShapeDtypeStruct((n, POOL_WIDTH), F32),
            jax.ShapeDtypeStruct((n, 3 * ATTN_WIDTH), BF16),
        ],
        compiler_params=_params(("arbitrary",), 40),
        name="ab_in",
    )(x, g, w_bf16)


ATTN_UNROLL = 4


def _attn_body(q_ref, k_ref, v_ref, o_ref, qf, kf, vf, o1, o4, o16, l1, l4, l16):
    qf[...] = q_ref[...].astype(F32)
    kf[...] = k_ref[...].astype(F32)
    vf[...] = v_ref[...].astype(F32)

    key = lax.broadcasted_iota(jnp.int32, (ATTN_BLOCK, HEADS_PER_STEP * ATTN_BLOCK), 0)
    qry = lax.broadcasted_iota(jnp.int32, (ATTN_BLOCK, HEADS_PER_STEP * ATTN_BLOCK), 1) % ATTN_BLOCK
    cur_ok = key <= qry
    prev_ok = key >= qry
    head0_lane = lax.broadcasted_iota(jnp.int32, (ATTN_BLOCK, LANES), 1) < HEAD_DIM

    def rows(ref, start, d):
        if d == 1:
            return ref[pl.ds(start, ATTN_BLOCK), :]
        return ref[pl.ds(start, ATTN_BLOCK, stride=d), :]

    def put(ref, start, d, val):
        if d == 1:
            ref[pl.ds(start, ATTN_BLOCK), :] = val
        else:
            ref[pl.ds(start, ATTN_BLOCK, stride=d), :] = val

    for d, o_acc, l_acc in ((1, o1, l1), (4, o4, l4), (16, o16, l16)):
        nb = SEQ // d // ATTN_BLOCK
        has_prev = nb > 1

        n_blocks = SEQ // ATTN_BLOCK
        halves = [slice(hh * ATTN_BLOCK, (hh + 1) * ATTN_BLOCK) for hh in range(HEADS_PER_STEP)]
        chans = [slice(hh * HEAD_DIM, (hh + 1) * HEAD_DIM) for hh in range(HEADS_PER_STEP)]

        def block_start(idx, d=d, nb=nb):
            return idx // nb + (idx % nb) * (ATTN_BLOCK * d)

        def scores(idx, k_prev, d=d, has_prev=has_prev):
            start = block_start(idx)
            qv = rows(qf, start, d)
            q2 = jnp.concatenate([jnp.where(head0_lane, qv, 0.0), jnp.where(head0_lane, 0.0, qv)],
                                 axis=0).astype(BF16)
            kc = rows(kf, start, d).astype(BF16)
            keys = jnp.concatenate([k_prev, kc], axis=0) if has_prev else kc
            return _dot_nt(keys, q2), kc

        def finish(pv, den, lse, start, d=d, o_acc=o_acc, l_acc=l_acc):
            out_t = jnp.concatenate([pv[c, h] / den[:, h] for c, h in zip(chans, halves)], axis=0)
            lse_t = jnp.concatenate([jnp.broadcast_to(lse[:, h], (HEAD_DIM, ATTN_BLOCK)) for h in halves],
                                    axis=0)
            put(o_acc, start, d, out_t.T)
            put(l_acc, start, d, lse_t.T)

        def block(idx, carry, d=d, nb=nb, has_prev=has_prev):
            s_raw, k_cur, v_prev_t, pv_last, den_last, lse_last, start_last = carry
            s_next, k_next = scores(jnp.minimum(idx + 1, n_blocks - 1), k_cur)
            finish(pv_last, den_last, lse_last, start_last)
            start = block_start(idx)
            vc_t = rows(vf, start, d).T.astype(BF16)
            if has_prev:
                vals_t = jnp.concatenate([v_prev_t, vc_t], axis=1)
                ok = jnp.concatenate([jnp.logical_and(prev_ok, idx % nb > 0), cur_ok], axis=0)
            else:
                vals_t, ok = vc_t, cur_ok
            s = jnp.where(ok, s_raw, NEG)
            m = jnp.max(s, axis=0, keepdims=True)
            p = jnp.exp(s - m)
            den = jnp.sum(p, axis=0, keepdims=True)
            pv = _dot(vals_t, p.astype(BF16))
            return s_next, k_next, vc_t, pv, den, m + jnp.log(den), start

        k_none = jnp.zeros((ATTN_BLOCK, LANES), BF16)
        s_first, k_first = scores(0, k_none)
        width = HEADS_PER_STEP * ATTN_BLOCK
        init = (s_first, k_first, jnp.zeros((LANES, ATTN_BLOCK), BF16), jnp.zeros((LANES, width), F32),
                jnp.ones((1, width), F32), jnp.zeros((1, width), F32), jnp.int32(0))
        last = lax.fori_loop(0, n_blocks, block, init, unroll=ATTN_UNROLL)
        finish(*last[3:])

    def merge(c, carry):
        sl = pl.ds(pl.multiple_of(c * ATTN_BLOCK, ATTN_BLOCK), ATTN_BLOCK)
        la, lb, lc = l1[sl, :], l4[sl, :], l16[sl, :]
        mx = jnp.maximum(jnp.maximum(la, lb), lc)
        wa, wb, wc = jnp.exp(la - mx), jnp.exp(lb - mx), jnp.exp(lc - mx)
        num = wa * o1[sl, :] + wb * o4[sl, :] + wc * o16[sl, :]
        o_ref[sl, :] = (num / (wa + wb + wc)).astype(o_ref.dtype)
        return carry

    lax.fori_loop(0, SEQ // ATTN_BLOCK, merge, 0)


def _attention(qkv):
    n_hp = ATTN_WIDTH // LANES
    blk = (None, SEQ, LANES)
    return pl.pallas_call(
        _attn_body,
        grid=(qkv.shape[0], n_hp),
        in_specs=[
            pl.BlockSpec(blk, lambda b, hp: (b, 0, hp)),
            pl.BlockSpec(blk, lambda b, hp: (b, 0, n_hp + hp)),
            pl.BlockSpec(blk, lambda b, hp: (b, 0, 2 * n_hp + hp)),
        ],
        out_specs=pl.BlockSpec(blk, lambda b, hp: (b, 0, hp)),
        out_shape=jax.ShapeDtypeStruct((qkv.shape[0], SEQ, ATTN_WIDTH), BF16),
        scratch_shapes=[pltpu.VMEM((SEQ, LANES), F32) for _ in range(9)],
        compiler_params=_params(("arbitrary", "arbitrary"), 32),
        name="dilated_attn",
    )(qkv, qkv, qkv)


AO_TS = 512


def _ab_out_body(zp_ref, halo_ref, o_ref, x_ref, pw_ref, ps_ref, w_ref, out_ref, zz):
    i = pl.program_id(1)
    zz[0:POOL_HALO, :] = jnp.where(i > 0, halo_ref[...], 0.0)
    zz[POOL_HALO:, :] = zp_ref[...]
    pos = i * AO_TS + lax.broadcasted_iota(jnp.int32, (AO_TS, 1), 0)
    parts = []
    for g, w in enumerate(POOL_WINDOWS):
        cols = slice(g * POOL_CH, (g + 1) * POOL_CH)
        acc = zz[pl.ds(POOL_HALO, AO_TS), cols]
        for j in range(1, w):
            acc = acc + zz[pl.ds(POOL_HALO - j, AO_TS), cols]
        count = jnp.minimum(pos + 1, w).astype(F32)
        pooled = acc / count - zp_ref[:, cols]
        parts.append(_dot(pooled.astype(BF16), pw_ref[g]))
    a = jnp.concatenate(parts, axis=1) * ps_ref[...]
    y = jnp.concatenate([a.astype(BF16), o_ref[...]], axis=1)
    out_ref[...] = x_ref[...] + _dot(y, w_ref[...])


def _ab_out(zp, o, x, pool_w_bf16, pool_scale, w_out_bf16):
    halo_per_tile = AO_TS // POOL_HALO
    return pl.pallas_call(
        _ab_out_body,
        grid=(BATCH, SEQ // AO_TS),
        in_specs=[
            pl.BlockSpec((None, AO_TS, POOL_WIDTH), lambda b, i: (b, i, 0)),
            pl.BlockSpec((None, POOL_HALO, POOL_WIDTH),
                         lambda b, i: (b, jnp.maximum(i * halo_per_tile - 1, 0), 0)),
            pl.BlockSpec((None, AO_TS, ATTN_WIDTH), lambda b, i: (b, i, 0)),
            pl.BlockSpec((None, AO_TS, D_MODEL), lambda b, i: (b, i, 0)),
            pl.BlockSpec((len(POOL_WINDOWS), POOL_CH, POOL_CH), lambda b, i: (0, 0, 0)),
            pl.BlockSpec((1, POOL_WIDTH), lambda b, i: (0, 0)),
            pl.BlockSpec((POOL_WIDTH + ATTN_WIDTH, D_MODEL), lambda b, i: (0, 0)),
        ],
        out_specs=pl.BlockSpec((None, AO_TS, D_MODEL), lambda b, i: (b, i, 0)),
        out_shape=jax.ShapeDtypeStruct((BATCH, SEQ, D_MODEL), F32),
        scratch_shapes=[pltpu.VMEM((AO_TS + POOL_HALO, POOL_WIDTH), F32)],
        compiler_params=_params(("arbitrary", "arbitrary"), 32),
        name="pool_ab_out",
    )(zp, zp, o, x, pool_w_bf16, pool_scale, w_out_bf16)


GM_TS = 256


def _gelu_tanh(x):
    return 0.5 * x * (1.0 + jnp.tanh(0.7978845608028654 * (x + 0.044715 * (x * x * x))))


def _gmlp_body(x_ref, g_ref, win_ref, gn_ref, ws_ref, bs_ref, wout_ref, out_ref):
    x = x_ref[...]
    h = _rms(x, g_ref[...]).astype(BF16)
    z = _gelu_tanh(_dot(h, win_ref[...]))
    u = z[:, :GM_WIDTH]
    v = z[:, GM_WIDTH:]
    vc = v - jnp.mean(v, axis=-1, keepdims=True)
    vn = (vc * lax.rsqrt(jnp.mean(vc * vc, axis=-1, keepdims=True) + EPS) * gn_ref[...]).astype(BF16)
    ri = lax.broadcasted_iota(jnp.int32, (GM_CHUNK, GM_CHUNK), 0)
    ci = lax.broadcasted_iota(jnp.int32, (GM_CHUNK, GM_CHUNK), 1)
    causal = ci <= ri
    rows = []
    for c in range(GM_TS // GM_CHUNK):
        rsl = slice(c * GM_CHUNK, (c + 1) * GM_CHUNK)
        cols = []
        for g in range(GM_GROUPS):
            csl = slice(g * (GM_WIDTH // GM_GROUPS), (g + 1) * (GM_WIDTH // GM_GROUPS))
            ws_c = jnp.where(causal, ws_ref[g], 0.0).astype(BF16)
            sv = _dot(ws_c, vn[rsl, csl]) + bs_ref[g]
            cols.append(u[rsl, csl] * sv)
        rows.append(jnp.concatenate(cols, axis=1))
    y = jnp.concatenate(rows, axis=0).astype(BF16)
    out_ref[...] = x + _dot(y, wout_ref[...])


def _gmlp(x, g, w_in_bf16, gm_norm, ws, bs_bcast, w_out_bf16):
    n = x.shape[0]
    const2 = lambda i: (0, 0)
    const3 = lambda i: (0, 0, 0)
    return pl.pallas_call(
        _gmlp_body,
        grid=(n // GM_TS,),
        in_specs=[
            pl.BlockSpec((GM_TS, D_MODEL), lambda i: (i, 0)),
            pl.BlockSpec((1, D_MODEL), const2),
            pl.BlockSpec((D_MODEL, 2 * GM_WIDTH), const2),
            pl.BlockSpec((1, GM_WIDTH), const2),
            pl.BlockSpec((GM_GROUPS, GM_CHUNK, GM_CHUNK), const3),
            pl.BlockSpec((GM_GROUPS, GM_CHUNK, GM_CHUNK), const3),
            pl.BlockSpec((GM_WIDTH, D_MODEL), const2),
        ],
        out_specs=pl.BlockSpec((GM_TS, D_MODEL), lambda i: (i, 0)),
        out_shape=jax.ShapeDtypeStruct((n, D_MODEL), F32),
        compiler_params=_params(("arbitrary",), 40),
        name="gmlp",
    )(x, g, w_in_bf16, gm_norm, ws, bs_bcast, w_out_bf16)


RT_TN = 512
ROUTER_LANES = LANES
EXPERT_LANE0 = N_GROUPS


def _router_body(x_ref, g_ref, wr_ref, br_ref, h_ref, ri_ref, rg_ref, cnt_ref, tri, running):
    step = pl.program_id(0)

    @pl.when(step == 0)
    def _():
        r = lax.broadcasted_iota(jnp.int32, (RT_TN, RT_TN), 0)
        c = lax.broadcasted_iota(jnp.int32, (RT_TN, RT_TN), 1)
        tri[...] = jnp.where(c < r, 1.0, 0.0).astype(BF16)
        running[...] = jnp.zeros_like(running)

    h = _rms(x_ref[...], g_ref[...])
    h_ref[...] = h.reshape(RT_TN, *TOKEN_TILE)
    logits = jnp.dot(h, wr_ref[...], preferred_element_type=F32,
                     precision=lax.Precision.HIGHEST) + br_ref[...]
    lane = lax.broadcasted_iota(jnp.int32, (RT_TN, ROUTER_LANES), 1)

    is_group = lane < N_GROUPS
    lg = jnp.where(is_group, logits, NEG)
    mg = jnp.max(lg, axis=1, keepdims=True)
    p_top = 1.0 / jnp.sum(jnp.where(is_group, jnp.exp(logits - mg), 0.0), axis=1, keepdims=True)
    g_idx = jnp.min(jnp.where(lg == mg, lane, ROUTER_LANES), axis=1, keepdims=True)

    in_group = jnp.logical_and(
        jnp.logical_and(lane >= EXPERT_LANE0, lane < EXPERT_LANE0 + N_EXPERTS),
        ((lane - EXPERT_LANE0) >> 3) == g_idx)
    le = jnp.where(in_group, logits, NEG)
    t1 = jnp.max(le, axis=1, keepdims=True)
    i1 = jnp.min(jnp.where(le == t1, lane, ROUTER_LANES), axis=1, keepdims=True)
    le2 = jnp.where(lane == i1, NEG, le)
    t2 = jnp.max(le2, axis=1, keepdims=True)
    i2 = jnp.min(jnp.where(le2 == t2, lane, ROUTER_LANES), axis=1, keepdims=True)
    e0 = i1 - EXPERT_LANE0
    e1 = i2 - EXPERT_LANE0
    ex = jnp.exp(t2 - t1)
    w0 = 1.0 / (1.0 + ex)
    g0 = p_top * w0
    g1 = p_top * (ex * w0)

    oh0 = lane == e0
    oh1 = lane == e1
    cnt = jnp.where(jnp.logical_or(oh0, oh1), 1.0, 0.0)
    base = running[...] + _dot(tri[...], cnt.astype(BF16))
    rank0 = jnp.sum(jnp.where(oh0, base, 0.0), axis=1, keepdims=True).astype(jnp.int32)
    rank1 = jnp.sum(jnp.where(oh1, base, 0.0), axis=1, keepdims=True).astype(jnp.int32)
    running[...] = running[...] + jnp.sum(cnt, axis=0, keepdims=True)
    cnt_ref[...] = running[...]

    ri_ref[...] = jnp.where(lane == 0, e0, jnp.where(lane == 1, e1,
                            jnp.where(lane == 2, rank0, jnp.where(lane == 3, rank1, 0))))
    rg_ref[...] = jnp.where(lane == 0, g0, jnp.where(lane == 1, g1, 0.0))


def _router(x, g, wr, br):
    n = x.shape[0]
    const2 = lambda i: (0, 0)
    return pl.pallas_call(
        _router_body,
        grid=(n // RT_TN,),
        in_specs=[
            pl.BlockSpec((RT_TN, D_MODEL), lambda i: (i, 0)),
            pl.BlockSpec((1, D_MODEL), const2),
            pl.BlockSpec((D_MODEL, ROUTER_LANES), const2),
            pl.BlockSpec((1, ROUTER_LANES), const2),
        ],
        out_specs=[
            pl.BlockSpec((RT_TN, *TOKEN_TILE), lambda i: (i, 0, 0)),
            pl.BlockSpec((RT_TN, ROUTER_LANES), lambda i: (i, 0)),
            pl.BlockSpec((RT_TN, ROUTER_LANES), lambda i: (i, 0)),
            pl.BlockSpec((1, ROUTER_LANES), const2),
        ],
        out_shape=[
            jax.ShapeDtypeStruct((n, *TOKEN_TILE), F32),
            jax.ShapeDtypeStruct((n, ROUTER_LANES), jnp.int32),
            jax.ShapeDtypeStruct((n, ROUTER_LANES), F32),
            jax.ShapeDtypeStruct((1, ROUTER_LANES), F32),
        ],
        scratch_shapes=[pltpu.VMEM((RT_TN, RT_TN), BF16), pltpu.VMEM((1, ROUTER_LANES), F32)],
        compiler_params=_params(("arbitrary",), 32),
        name="router",
    )(x, g, wr, br)


DP_TD = 256
N_DMA_QUEUES = 2


def _dispatch_body(pad_start, n_pad, n_used, dest_ref, h_ref, xs_hbm, zblk, sem, pad_sem):
    step = pl.program_id(0)

    def row_copy(t, dst):
        return pltpu.make_async_copy(h_ref.at[t], xs_hbm.at[dst], sem)

    def pad_copy(dst):
        return pltpu.make_async_copy(zblk.at[0], xs_hbm.at[dst], pad_sem)

    def tail_copy(blk):
        return pltpu.make_async_copy(zblk, xs_hbm.at[pl.ds(blk * ROW_BLOCK, ROW_BLOCK)], pad_sem)

    @pl.when(step == 0)
    def _():
        zblk[...] = jnp.zeros_like(zblk)

        def tail(blk, carry):
            tail_copy(blk).start()
            tail_copy(blk).wait()
            return carry

        lax.fori_loop(n_used[0], N_BLK, tail, 0)

        def per_expert(e, carry):
            def start(j, c):
                pad_copy(pad_start[e] + j).start()
                return c
            lax.fori_loop(0, n_pad[e], start, 0)

            def wait(j, c):
                pad_copy(0).wait()
                return c
            lax.fori_loop(0, n_pad[e], wait, 0)
            return carry

        lax.fori_loop(0, N_EXPERTS, per_expert, 0)

    for t in range(DP_TD):
        for s in range(TOP_K):
            row_copy(t, dest_ref[0, 0, TOP_K * t + s]).start(priority=s % N_DMA_QUEUES)

    def wait(k, carry):
        row_copy(0, 0).wait()
        return carry

    lax.fori_loop(0, TOP_K * DP_TD, wait, 0)


def _dispatch(pad_start, n_pad, n_used, dest, h):
    n = h.shape[0]
    dest3 = dest.reshape(n // DP_TD, 1, TOP_K * DP_TD)
    return pl.pallas_call(
        _dispatch_body,
        grid_spec=pltpu.PrefetchScalarGridSpec(
            num_scalar_prefetch=3,
            grid=(n // DP_TD,),
            in_specs=[
                pl.BlockSpec((1, 1, TOP_K * DP_TD), lambda i, ps, npd, nu: (i, 0, 0),
                             memory_space=pltpu.SMEM),
                pl.BlockSpec((DP_TD, *TOKEN_TILE), lambda i, ps, npd, nu: (i, 0, 0)),
            ],
            out_specs=pl.BlockSpec(memory_space=pl.ANY),
            scratch_shapes=[pltpu.VMEM((ROW_BLOCK, *TOKEN_TILE), F32), pltpu.SemaphoreType.DMA,
                            pltpu.SemaphoreType.DMA],
        ),
        out_shape=jax.ShapeDtypeStruct((N_ROWS, *TOKEN_TILE), F32),
        compiler_params=_params(("arbitrary",), 16),
        name="moe_dispatch",
    )(pad_start, n_pad, n_used, dest3, h)


N_STAGE = 3
WEIGHT_DMA_PRIORITY = 1
X_SLOTS = 4
Y_SLOTS = 4


def _expert_body(layer, blk_expert, next_expert, n_used, xs_hbm, wg_hbm, wu_hbm, wd_hbm, y_hbm,
                 xbuf, ybuf, sg, su, sd, wg, wu, wd, slot_ref, wsem, xsem, ysem):
    i = pl.program_id(0)
    e = blk_expert[i]
    nu = n_used[0]

    def x_copy(blk, slot):
        return pltpu.make_async_copy(xs_hbm.at[pl.ds(blk * ROW_BLOCK, ROW_BLOCK)], xbuf.at[slot],
                                     xsem.at[slot])

    def y_copy(blk, slot):
        return pltpu.make_async_copy(ybuf.at[slot], y_hbm.at[pl.ds(blk * ROW_BLOCK, ROW_BLOCK)],
                                     ysem.at[slot])

    def fetch(expert, slot):
        return (pltpu.make_async_copy(wg_hbm.at[layer, expert], sg.at[slot], wsem.at[slot, 0]),
                pltpu.make_async_copy(wu_hbm.at[layer, expert], su.at[slot], wsem.at[slot, 1]),
                pltpu.make_async_copy(wd_hbm.at[layer, expert], sd.at[slot], wsem.at[slot, 2]))

    def start_fetch(expert, slot):
        for c in fetch(expert, slot):
            c.start(priority=WEIGHT_DMA_PRIORITY)

    nxt = next_expert[e]
    nxt2 = next_expert[nxt]

    @pl.when(i == 0)
    def _():
        slot_ref[0] = 0
        start_fetch(e, 0)

        @pl.when(nxt != e)
        def _():
            start_fetch(nxt, 1)

        for b in range(X_SLOTS - 1):
            @pl.when(b < nu)
            def _():
                x_copy(b, b).start()

    used = i < nu
    first_of_expert = jnp.logical_or(i == 0, blk_expert[jnp.maximum(i - 1, 0)] != e)
    xslot = i % X_SLOTS
    yslot = i % Y_SLOTS

    @pl.when(i >= Y_SLOTS)
    def _():
        y_copy(i - Y_SLOTS, yslot).wait()

    @pl.when(jnp.logical_and(used, first_of_expert))
    def _():
        slot = slot_ref[0]

        @pl.when(nxt2 != nxt)
        def _():
            start_fetch(nxt2, (slot + N_STAGE - 1) % N_STAGE)

        for c in fetch(e, slot):
            c.wait()
        wg[...] = sg[slot].astype(BF16)
        wu[...] = su[slot].astype(BF16)
        wd[...] = sd[slot].astype(BF16)
        slot_ref[0] = (slot + 1) % N_STAGE

    @pl.when(used)
    def _():
        ahead = i + X_SLOTS - 1

        @pl.when(ahead < nu)
        def _():
            x_copy(ahead, ahead % X_SLOTS).start()

        x_copy(i, xslot).wait()
        x = xbuf[xslot].reshape(ROW_BLOCK, D_MODEL).astype(BF16)
        a = _dot(x, wg[...])
        b = _dot(x, wu[...])
        hmid = (a * (1.0 / (1.0 + jnp.exp(-a)))) * b
        ybuf[yslot] = _dot(hmid.astype(BF16), wd[...]).reshape(ROW_BLOCK, *TOKEN_TILE)

    @pl.when(jnp.logical_not(used))
    def _():
        ybuf[yslot] = jnp.zeros((ROW_BLOCK, *TOKEN_TILE), F32)

    y_copy(i, yslot).start()

    @pl.when(i == N_BLK - 1)
    def _():
        for s in range(Y_SLOTS):
            y_copy(0, s).wait()


def _experts(layer, blk_expert, next_expert, n_used, xs, w_gate, w_up, w_down):
    assert N_BLK >= Y_SLOTS
    hbm = pl.BlockSpec(memory_space=pl.ANY)
    return pl.pallas_call(
        functools.partial(_expert_body, layer),
        grid_spec=pltpu.PrefetchScalarGridSpec(
            num_scalar_prefetch=3,
            grid=(N_BLK,),
            in_specs=[hbm, hbm, hbm, hbm],
            out_specs=hbm,
            scratch_shapes=[
                pltpu.VMEM((X_SLOTS, ROW_BLOCK, *TOKEN_TILE), F32),
                pltpu.VMEM((Y_SLOTS, ROW_BLOCK, *TOKEN_TILE), F32),
                pltpu.VMEM((N_STAGE, D_MODEL, D_EXPERT), F32),
                pltpu.VMEM((N_STAGE, D_MODEL, D_EXPERT), F32),
                pltpu.VMEM((N_STAGE, D_EXPERT, D_MODEL), F32),
                pltpu.VMEM((D_MODEL, D_EXPERT), BF16),
                pltpu.VMEM((D_MODEL, D_EXPERT), BF16),
                pltpu.VMEM((D_EXPERT, D_MODEL), BF16),
                pltpu.SMEM((1,), jnp.int32),
                pltpu.SemaphoreType.DMA((N_STAGE, 3)),
                pltpu.SemaphoreType.DMA((X_SLOTS,)),
                pltpu.SemaphoreType.DMA((Y_SLOTS,)),
            ],
        ),
        out_shape=jax.ShapeDtypeStruct((N_ROWS, *TOKEN_TILE), F32),
        compiler_params=_params(("arbitrary",), 40),
        name="moe_experts",
    )(blk_expert, next_expert, n_used, xs, w_gate, w_up, w_down)


CB_TC = 256


def _combine_body(final, dest_ref, dest_next_ref, y_hbm, x_ref, rg_ref, fg_ref, out_ref, ybuf, sem):
    i = pl.program_id(0)
    n_steps = pl.num_programs(0)
    slot = i % 2

    def row_copy(src, buf, s, t):
        return pltpu.make_async_copy(y_hbm.at[src], ybuf.at[buf, s, t], sem.at[buf])

    def issue(idx_ref, buf):
        for t in range(CB_TC):
            for s in range(TOP_K):
                row_copy(idx_ref[0, 0, TOP_K * t + s], buf, s, t).start(priority=s % N_DMA_QUEUES)

    @pl.when(i == 0)
    def _():
        issue(dest_ref, 0)

    @pl.when(i + 1 < n_steps)
    def _():
        issue(dest_next_ref, 1 - slot)

    def wait(k, carry):
        row_copy(0, slot, 0, 0).wait()
        return carry

    lax.fori_loop(0, TOP_K * CB_TC, wait, 0)
    rg = rg_ref[...]
    y0 = ybuf[slot, 0].reshape(CB_TC, D_MODEL)
    y1 = ybuf[slot, 1].reshape(CB_TC, D_MODEL)
    out = x_ref[...] + rg[:, 0:1] * y0 + rg[:, 1:2] * y1
    out_ref[...] = _rms(out, fg_ref[...]) if final else out


def _combine(dest, y, x, rg, final_g, final):
    n = x.shape[0]
    n_steps = n // CB_TC
    dest3 = dest.reshape(n_steps, 1, TOP_K * CB_TC)
    return pl.pallas_call(
        functools.partial(_combine_body, final),
        grid=(n_steps,),
        in_specs=[
            pl.BlockSpec((1, 1, TOP_K * CB_TC), lambda i: (i, 0, 0), memory_space=pltpu.SMEM),
            pl.BlockSpec((1, 1, TOP_K * CB_TC), lambda i: (jnp.minimum(i + 1, n_steps - 1), 0, 0),
                         memory_space=pltpu.SMEM),
            pl.BlockSpec(memory_space=pl.ANY),
            pl.BlockSpec((CB_TC, D_MODEL), lambda i: (i, 0)),
            pl.BlockSpec((CB_TC, ROUTER_LANES), lambda i: (i, 0)),
            pl.BlockSpec((1, D_MODEL), lambda i: (0, 0)),
        ],
        out_specs=pl.BlockSpec((CB_TC, D_MODEL), lambda i: (i, 0)),
        out_shape=jax.ShapeDtypeStruct((n, D_MODEL), F32),
        scratch_shapes=[pltpu.VMEM((2, TOP_K, CB_TC, *TOKEN_TILE), F32), pltpu.SemaphoreType.DMA((2,))],
        compiler_params=_params(("arbitrary",), 16),
        name="moe_combine",
    )(dest3, dest3, y, x, rg, final_g)


def _moe(layer, x, norm_g, group_w, group_b, exp_w, exp_b, w_gate, w_up, w_down, final_g, final):
    pad_w = ROUTER_LANES - N_GROUPS - N_EXPERTS
    wr = jnp.concatenate([group_w, exp_w, jnp.zeros((D_MODEL, pad_w), F32)], axis=1)
    br = jnp.concatenate([group_b, exp_b, jnp.zeros((pad_w,), F32)])[None, :]
    h, ri, rg, cnt = _router(x, norm_g[None, :], wr, br)

    experts = jnp.arange(N_EXPERTS, dtype=jnp.int32)
    counts = cnt[0, :N_EXPERTS].astype(jnp.int32)
    pcounts = (counts + ROW_BLOCK - 1) // ROW_BLOCK * ROW_BLOCK
    pends = jnp.cumsum(pcounts)
    pstarts = pends - pcounts
    start_of = jnp.sum(jnp.where(ri[:, 0:TOP_K, None] == experts, pstarts, 0), axis=-1)
    dest = (start_of + ri[:, TOP_K:2 * TOP_K]).astype(jnp.int32)
    blk_row0 = jnp.arange(N_BLK, dtype=jnp.int32) * ROW_BLOCK
    blk_expert = jnp.minimum(jnp.sum(pends[None, :] <= blk_row0[:, None], axis=1),
                             N_EXPERTS - 1).astype(jnp.int32)
    n_used = (pends[-1:] // ROW_BLOCK).astype(jnp.int32)
    later_used = jnp.logical_and(experts[None, :] > experts[:, None], (pcounts > 0)[None, :])
    next_expert = jnp.min(jnp.where(later_used, experts[None, :], N_EXPERTS), axis=1)
    next_expert = jnp.where(next_expert == N_EXPERTS, experts, next_expert).astype(jnp.int32)

    xs = _dispatch((pstarts + counts).astype(jnp.int32), (pcounts - counts).astype(jnp.int32),
                   n_used, dest, h)
    y = _experts(layer, blk_expert, next_expert, n_used, xs, w_gate, w_up, w_down)
    return _combine(dest, y, x, rg, final_g, final)


def kernel(x, norm_mix, norm_ffn, norm_final, ab_w_in, ab_w_out, pool_w, pool_scale, gm_w_in, gm_norm, gm_ws, gm_bs, gm_w_out, router_group_w, router_group_b, router_expert_w, router_expert_b, moe_w_gate, moe_w_up, moe_w_down):
    assert x.shape == (BATCH, SEQ, D_MODEL) and x.dtype == F32
    xf = x.reshape(N_TOK, D_MODEL)
    for layer in range(DEPTH):
        i = layer // 2
        if layer % 2 == 0:
            zp, qkv = _ab_in(xf, norm_mix[layer][None, :], ab_w_in[i].astype(BF16))
            o = _attention(qkv.reshape(BATCH, SEQ, 3 * ATTN_WIDTH))
            xf = _ab_out(zp.reshape(BATCH, SEQ, POOL_WIDTH), o, xf.reshape(BATCH, SEQ, D_MODEL),
                         pool_w[i].astype(BF16), pool_scale[i][None, :],
                         ab_w_out[i].astype(BF16)).reshape(N_TOK, D_MODEL)
        else:
            bs_bcast = jnp.broadcast_to(gm_bs[i][:, :, None], (GM_GROUPS, GM_CHUNK, GM_CHUNK))
            xf = _gmlp(xf, norm_mix[layer][None, :], gm_w_in[i].astype(BF16), gm_norm[i][None, :],
                       gm_ws[i], bs_bcast, gm_w_out[i].astype(BF16))
        xf = _moe(layer, xf, norm_ffn[layer], router_group_w[layer], router_group_b[layer],
                  router_expert_w[layer], router_expert_b[layer], moe_w_gate, moe_w_up, moe_w_down,
                  norm_final[None, :], layer == DEPTH - 1)
    return xf.reshape(BATCH, SEQ, D_MODEL)
```

```python
import functools

import jax
import jax.numpy as jnp
from jax import lax
from jax.experimental import pallas as pl
from jax.experimental.pallas import tpu as pltpu
from jax.experimental.pallas import tpu_sc as plsc

D_MODEL = 1024
BATCH = 8
SEQ = 2048
DEPTH = 4
N_TOK = BATCH * SEQ

POOL_WINDOWS = (2, 4, 8, 16)
POOL_CH = 128
POOL_WIDTH = 512
POOL_HALO = 16
HEAD_DIM = 64
ATTN_WIDTH = 512
HEADS_PER_STEP = 2
ATTN_BLOCK = 128
GM_CHUNK = 128
GM_GROUPS = 8
GM_WIDTH = 1024
N_GROUPS = 8
N_EXPERTS = 64
TOP_K = 2
D_EXPERT = 512
ROW_BLOCK = 128
N_ROWS = (N_TOK * TOP_K + N_EXPERTS * (ROW_BLOCK - 1) + ROW_BLOCK - 1) // ROW_BLOCK * ROW_BLOCK
N_BLK = N_ROWS // ROW_BLOCK
EPS = 1e-6
NEG = -1e30

LANES = 128
SUBLANES = 8
TOKEN_TILE = (SUBLANES, LANES)
assert SUBLANES * LANES == D_MODEL
V7X_VMEM_BYTES = 64 * 1024 * 1024
MIB = 1024 * 1024

F32 = jnp.float32
BF16 = jnp.bfloat16


def _params(semantics, vmem_mib):
    assert vmem_mib * MIB < V7X_VMEM_BYTES
    return pltpu.CompilerParams(dimension_semantics=semantics, vmem_limit_bytes=vmem_mib * MIB)


def _rms(x, g):
    return x * lax.rsqrt(jnp.mean(x * x, axis=-1, keepdims=True) + EPS) * g


def _dot(a, b):
    return jnp.dot(a, b, preferred_element_type=F32)


def _dot_nt(a, b):
    return lax.dot_general(a, b, (((1,), (1,)), ((), ())), preferred_element_type=F32)


AB_TM = 512


def _ab_in_body(x_ref, g_ref, w_ref, zp_ref, qkv_ref):
    h = _rms(x_ref[...], g_ref[...]).astype(BF16)
    z = _dot(h, w_ref[...])
    zp_ref[...] = z[:, :POOL_WIDTH]
    q = z[:, POOL_WIDTH:POOL_WIDTH + ATTN_WIDTH] * (HEAD_DIM ** -0.5)
    qkv_ref[...] = jnp.concatenate([q, z[:, POOL_WIDTH + ATTN_WIDTH:]], axis=1).astype(BF16)


def _ab_in(x, g, w_bf16):
    n = x.shape[0]
    return pl.pallas_call(
        _ab_in_body,
        grid=(n // AB_TM,),
        in_specs=[
            pl.BlockSpec((AB_TM, D_MODEL), lambda i: (i, 0)),
            pl.BlockSpec((1, D_MODEL), lambda i: (0, 0)),
            pl.BlockSpec((D_MODEL, 4 * ATTN_WIDTH), lambda i: (0, 0)),
        ],
        out_specs=[
            pl.BlockSpec((AB_TM, POOL_WIDTH), lambda i: (i, 0)),
            pl.BlockSpec((AB_TM, 3 * ATTN_WIDTH), lambda i: (i, 0)),
        ],
        out_shape=[
            jax.ShapeDtypeStruct((n, POOL_WIDTH), F32),
            jax.ShapeDtypeStruct((n, 3 * ATTN_WIDTH), BF16),
        ],
        compiler_params=_params(("arbitrary",), 40),
        name="ab_in",
    )(x, g, w_bf16)


ATTN_UNROLL = 4


def _attn_body(q_ref, k_ref, v_ref, o_ref, qf, kf, vf, o1, o4, o16, l1, l4, l16):
    qf[...] = q_ref[...].astype(F32)
    kf[...] = k_ref[...].astype(F32)
    vf[...] = v_ref[...].astype(F32)

    key = lax.broadcasted_iota(jnp.int32, (ATTN_BLOCK, HEADS_PER_STEP * ATTN_BLOCK), 0)
    qry = lax.broadcasted_iota(jnp.int32, (ATTN_BLOCK, HEADS_PER_STEP * ATTN_BLOCK), 1) % ATTN_BLOCK
    cur_ok = key <= qry
    prev_ok = key >= qry
    head0_lane = lax.broadcasted_iota(jnp.int32, (ATTN_BLOCK, LANES), 1) < HEAD_DIM

    def rows(ref, start, d):
        if d == 1:
            return ref[pl.ds(start, ATTN_BLOCK), :]
        return ref[pl.ds(start, ATTN_BLOCK, stride=d), :]

    def put(ref, start, d, val):
        if d == 1:
            ref[pl.ds(start, ATTN_BLOCK), :] = val
        else:
            ref[pl.ds(start, ATTN_BLOCK, stride=d), :] = val

    for d, o_acc, l_acc in ((1, o1, l1), (4, o4, l4), (16, o16, l16)):
        nb = SEQ // d // ATTN_BLOCK
        has_prev = nb > 1

        n_blocks = SEQ // ATTN_BLOCK
        halves = [slice(hh * ATTN_BLOCK, (hh + 1) * ATTN_BLOCK) for hh in range(HEADS_PER_STEP)]
        chans = [slice(hh * HEAD_DIM, (hh + 1) * HEAD_DIM) for hh in range(HEADS_PER_STEP)]

        def block_start(idx, d=d, nb=nb):
            return idx // nb + (idx % nb) * (ATTN_BLOCK * d)

        def scores(idx, k_prev, d=d, has_prev=has_prev):
            start = block_start(idx)
            qv = rows(qf, start, d)
            q2 = jnp.concatenate([jnp.where(head0_lane, qv, 0.0), jnp.where(head0_lane, 0.0, qv)],
                                 axis=0).astype(BF16)
            kc = rows(kf, start, d).astype(BF16)
            keys = jnp.concatenate([k_prev, kc], axis=0) if has_prev else kc
            return _dot_nt(keys, q2), kc

        def finish(pv, den, lse, start, d=d, o_acc=o_acc, l_acc=l_acc):
            out_t = jnp.concatenate([pv[c, h] / den[:, h] for c, h in zip(chans, halves)], axis=0)
            lse_t = jnp.concatenate([jnp.broadcast_to(lse[:, h], (HEAD_DIM, ATTN_BLOCK)) for h in halves],
                                    axis=0)
            put(o_acc, start, d, out_t.T)
            put(l_acc, start, d, lse_t.T)

        def block(idx, carry, d=d, nb=nb, has_prev=has_prev):
            s_raw, k_cur, v_prev_t, pv_last, den_last, lse_last, start_last = carry
            s_next, k_next = scores(jnp.minimum(idx + 1, n_blocks - 1), k_cur)
            finish(pv_last, den_last, lse_last, start_last)
            start = block_start(idx)
            vc_t = rows(vf, start, d).T.astype(BF16)
            if has_prev:
                vals_t = jnp.concatenate([v_prev_t, vc_t], axis=1)
                ok = jnp.concatenate([jnp.logical_and(prev_ok, idx % nb > 0), cur_ok], axis=0)
            else:
                vals_t, ok = vc_t, cur_ok
            s = jnp.where(ok, s_raw, NEG)
            m = jnp.max(s, axis=0, keepdims=True)
            p = jnp.exp(s - m)
            den = jnp.sum(p, axis=0, keepdims=True)
            pv = _dot(vals_t, p.astype(BF16))
            return s_next, k_next, vc_t, pv, den, m + jnp.log(den), start

        k_none = jnp.zeros((ATTN_BLOCK, LANES), BF16)
        s_first, k_first = scores(0, k_none)
        width = HEADS_PER_STEP * ATTN_BLOCK
        init = (s_first, k_first, jnp.zeros((LANES, ATTN_BLOCK), BF16), jnp.zeros((LANES, width), F32),
                jnp.ones((1, width), F32), jnp.zeros((1, width), F32), jnp.int32(0))
        last = lax.fori_loop(0, n_blocks, block, init, unroll=ATTN_UNROLL)
        finish(*last[3:])

    def merge(c, carry):
        sl = pl.ds(pl.multiple_of(c * ATTN_BLOCK, ATTN_BLOCK), ATTN_BLOCK)
        la, lb, lc = l1[sl, :], l4[sl, :], l16[sl, :]
        mx = jnp.maximum(jnp.maximum(la, lb), lc)
        wa, wb, wc = jnp.exp(la - mx), jnp.exp(lb - mx), jnp.exp(lc - mx)
        num = wa * o1[sl, :] + wb * o4[sl, :] + wc * o16[sl, :]
        o_ref[sl, :] = (num / (wa + wb + wc)).astype(o_ref.dtype)
        return carry

    lax.fori_loop(0, SEQ // ATTN_BLOCK, merge, 0)


def _attention(qkv):
    n_hp = ATTN_WIDTH // LANES
    blk = (None, SEQ, LANES)
    return pl.pallas_call(
        _attn_body,
        grid=(qkv.shape[0], n_hp),
        in_specs=[
            pl.BlockSpec(blk, lambda b, hp: (b, 0, hp)),
            pl.BlockSpec(blk, lambda b, hp: (b, 0, n_hp + hp)),
            pl.BlockSpec(blk, lambda b, hp: (b, 0, 2 * n_hp + hp)),
        ],
        out_specs=pl.BlockSpec(blk, lambda b, hp: (b, 0, hp)),
        out_shape=jax.ShapeDtypeStruct((qkv.shape[0], SEQ, ATTN_WIDTH), BF16),
        scratch_shapes=[pltpu.VMEM((SEQ, LANES), F32) for _ in range(9)],
        compiler_params=_params(("arbitrary", "arbitrary"), 32),
        name="dilated_attn",
    )(qkv, qkv, qkv)


AO_TS = 512


def _ab_out_body(zp_ref, halo_ref, o_ref, x_ref, pw_ref, ps_ref, w_ref, out_ref, zz):
    i = pl.program_id(1)
    zz[0:POOL_HALO, :] = jnp.where(i > 0, halo_ref[...], 0.0)
    zz[POOL_HALO:, :] = zp_ref[...]
    pos = i * AO_TS + lax.broadcasted_iota(jnp.int32, (AO_TS, 1), 0)
    parts = []
    for g, w in enumerate(POOL_WINDOWS):
        cols = slice(g * POOL_CH, (g + 1) * POOL_CH)
        acc = zz[pl.ds(POOL_HALO, AO_TS), cols]
        for j in range(1, w):
            acc = acc + zz[pl.ds(POOL_HALO - j, AO_TS), cols]
        count = jnp.minimum(pos + 1, w).astype(F32)
        pooled = acc / count - zp_ref[:, cols]
        parts.append(_dot(pooled.astype(BF16), pw_ref[g]))
    a = jnp.concatenate(parts, axis=1) * ps_ref[...]
    y = jnp.concatenate([a.astype(BF16), o_ref[...]], axis=1)
    out_ref[...] = x_ref[...] + _dot(y, w_ref[...])


def _ab_out(zp, o, x, pool_w_bf16, pool_scale, w_out_bf16):
    halo_per_tile = AO_TS // POOL_HALO
    return pl.pallas_call(
        _ab_out_body,
        grid=(BATCH, SEQ // AO_TS),
        in_specs=[
            pl.BlockSpec((None, AO_TS, POOL_WIDTH), lambda b, i: (b, i, 0)),
            pl.BlockSpec((None, POOL_HALO, POOL_WIDTH),
                         lambda b, i: (b, jnp.maximum(i * halo_per_tile - 1, 0), 0)),
            pl.BlockSpec((None, AO_TS, ATTN_WIDTH), lambda b, i: (b, i, 0)),
            pl.BlockSpec((None, AO_TS, D_MODEL), lambda b, i: (b, i, 0)),
            pl.BlockSpec((len(POOL_WINDOWS), POOL_CH, POOL_CH), lambda b, i: (0, 0, 0)),
            pl.BlockSpec((1, POOL_WIDTH), lambda b, i: (0, 0)),
            pl.BlockSpec((POOL_WIDTH + ATTN_WIDTH, D_MODEL), lambda b, i: (0, 0)),
        ],
        out_specs=pl.BlockSpec((None, AO_TS, D_MODEL), lambda b, i: (b, i, 0)),
        out_shape=jax.ShapeDtypeStruct((BATCH, SEQ, D_MODEL), F32),
        scratch_shapes=[pltpu.VMEM((AO_TS + POOL_HALO, POOL_WIDTH), F32)],
        compiler_params=_params(("arbitrary", "arbitrary"), 32),
        name="pool_ab_out",
    )(zp, zp, o, x, pool_w_bf16, pool_scale, w_out_bf16)


GM_TS = 256


def _gelu_tanh(x):
    return 0.5 * x * (1.0 + jnp.tanh(0.7978845608028654 * (x + 0.044715 * (x * x * x))))


def _gmlp_body(x_ref, g_ref, win_ref, gn_ref, ws_ref, bs_ref, wout_ref, out_ref):
    x = x_ref[...]
    h = _rms(x, g_ref[...]).astype(BF16)
    z = _gelu_tanh(_dot(h, win_ref[...]))
    u = z[:, :GM_WIDTH]
    v = z[:, GM_WIDTH:]
    vc = v - jnp.mean(v, axis=-1, keepdims=True)
    vn = (vc * lax.rsqrt(jnp.mean(vc * vc, axis=-1, keepdims=True) + EPS) * gn_ref[...]).astype(BF16)
    ri = lax.broadcasted_iota(jnp.int32, (GM_CHUNK, GM_CHUNK), 0)
    ci = lax.broadcasted_iota(jnp.int32, (GM_CHUNK, GM_CHUNK), 1)
    causal = ci <= ri
    rows = []
    for c in range(GM_TS // GM_CHUNK):
        rsl = slice(c * GM_CHUNK, (c + 1) * GM_CHUNK)
        cols = []
        for g in range(GM_GROUPS):
            csl = slice(g * (GM_WIDTH // GM_GROUPS), (g + 1) * (GM_WIDTH // GM_GROUPS))
            ws_c = jnp.where(causal, ws_ref[g], 0.0).astype(BF16)
            sv = _dot(ws_c, vn[rsl, csl]) + bs_ref[g]
            cols.append(u[rsl, csl] * sv)
        rows.append(jnp.concatenate(cols, axis=1))
    y = jnp.concatenate(rows, axis=0).astype(BF16)
    out_ref[...] = x + _dot(y, wout_ref[...])


def _gmlp(x, g, w_in_bf16, gm_norm, ws, bs_bcast, w_out_bf16):
    n = x.shape[0]
    const2 = lambda i: (0, 0)
    const3 = lambda i: (0, 0, 0)
    return pl.pallas_call(
        _gmlp_body,
        grid=(n // GM_TS,),
        in_specs=[
            pl.BlockSpec((GM_TS, D_MODEL), lambda i: (i, 0)),
            pl.BlockSpec((1, D_MODEL), const2),
            pl.BlockSpec((D_MODEL, 2 * GM_WIDTH), const2),
            pl.BlockSpec((1, GM_WIDTH), const2),
            pl.BlockSpec((GM_GROUPS, GM_CHUNK, GM_CHUNK), const3),
            pl.BlockSpec((GM_GROUPS, GM_CHUNK, GM_CHUNK), const3),
            pl.BlockSpec((GM_WIDTH, D_MODEL), const2),
        ],
        out_specs=pl.BlockSpec((GM_TS, D_MODEL), lambda i: (i, 0)),
        out_shape=jax.ShapeDtypeStruct((n, D_MODEL), F32),
        compiler_params=_params(("arbitrary",), 40),
        name="gmlp",
    )(x, g, w_in_bf16, gm_norm, ws, bs_bcast, w_out_bf16)


RT_TN = 512
ROUTER_LANES = LANES
EXPERT_LANE0 = N_GROUPS


def _router_body(x_ref, g_ref, wr_ref, br_ref, h_ref, ri_ref, rg_ref, cnt_ref, tri, running):
    step = pl.program_id(0)

    @pl.when(step == 0)
    def _():
        r = lax.broadcasted_iota(jnp.int32, (RT_TN, RT_TN), 0)
        c = lax.broadcasted_iota(jnp.int32, (RT_TN, RT_TN), 1)
        tri[...] = jnp.where(c < r, 1.0, 0.0).astype(BF16)
        running[...] = jnp.zeros_like(running)

    h = _rms(x_ref[...], g_ref[...])
    h_ref[...] = h.reshape(RT_TN, *TOKEN_TILE)
    logits = jnp.dot(h, wr_ref[...], preferred_element_type=F32,
                     precision=lax.Precision.HIGHEST) + br_ref[...]
    lane = lax.broadcasted_iota(jnp.int32, (RT_TN, ROUTER_LANES), 1)

    is_group = lane < N_GROUPS
    lg = jnp.where(is_group, logits, NEG)
    mg = jnp.max(lg, axis=1, keepdims=True)
    p_top = 1.0 / jnp.sum(jnp.where(is_group, jnp.exp(logits - mg), 0.0), axis=1, keepdims=True)
    g_idx = jnp.min(jnp.where(lg == mg, lane, ROUTER_LANES), axis=1, keepdims=True)

    in_group = jnp.logical_and(
        jnp.logical_and(lane >= EXPERT_LANE0, lane < EXPERT_LANE0 + N_EXPERTS),
        ((lane - EXPERT_LANE0) >> 3) == g_idx)
    le = jnp.where(in_group, logits, NEG)
    t1 = jnp.max(le, axis=1, keepdims=True)
    i1 = jnp.min(jnp.where(le == t1, lane, ROUTER_LANES), axis=1, keepdims=True)
    le2 = jnp.where(lane == i1, NEG, le)
    t2 = jnp.max(le2, axis=1, keepdims=True)
    i2 = jnp.min(jnp.where(le2 == t2, lane, ROUTER_LANES), axis=1, keepdims=True)
    e0 = i1 - EXPERT_LANE0
    e1 = i2 - EXPERT_LANE0
    ex = jnp.exp(t2 - t1)
    w0 = 1.0 / (1.0 + ex)
    g0 = p_top * w0
    g1 = p_top * (ex * w0)

    oh0 = lane == e0
    oh1 = lane == e1
    cnt = jnp.where(jnp.logical_or(oh0, oh1), 1.0, 0.0)
    base = running[...] + _dot(tri[...], cnt.astype(BF16))
    rank0 = jnp.sum(jnp.where(oh0, base, 0.0), axis=1, keepdims=True).astype(jnp.int32)
    rank1 = jnp.sum(jnp.where(oh1, base, 0.0), axis=1, keepdims=True).astype(jnp.int32)
    running[...] = running[...] + jnp.sum(cnt, axis=0, keepdims=True)
    cnt_ref[...] = running[...]

    ri_ref[...] = jnp.where(lane == 0, e0, jnp.where(lane == 1, e1,
                            jnp.where(lane == 2, rank0, jnp.where(lane == 3, rank1, 0))))
    rg_ref[...] = jnp.where(lane == 0, g0, jnp.where(lane == 1, g1, 0.0))


def _router(x, g, wr, br):
    n = x.shape[0]
    const2 = lambda i: (0, 0)
    return pl.pallas_call(
        _router_body,
        grid=(n // RT_TN,),
        in_specs=[
            pl.BlockSpec((RT_TN, D_MODEL), lambda i: (i, 0)),
            pl.BlockSpec((1, D_MODEL), const2),
            pl.BlockSpec((D_MODEL, ROUTER_LANES), const2),
            pl.BlockSpec((1, ROUTER_LANES), const2),
        ],
        out_specs=[
            pl.BlockSpec((RT_TN, *TOKEN_TILE), lambda i: (i, 0, 0)),
            pl.BlockSpec((RT_TN, ROUTER_LANES), lambda i: (i, 0)),
            pl.BlockSpec((RT_TN, ROUTER_LANES), lambda i: (i, 0)),
            pl.BlockSpec((1, ROUTER_LANES), const2),
        ],
        out_shape=[
            jax.ShapeDtypeStruct((n, *TOKEN_TILE), F32),
            jax.ShapeDtypeStruct((n, ROUTER_LANES), jnp.int32),
            jax.ShapeDtypeStruct((n, ROUTER_LANES), F32),
            jax.ShapeDtypeStruct((1, ROUTER_LANES), F32),
        ],
        scratch_shapes=[pltpu.VMEM((RT_TN, RT_TN), BF16), pltpu.VMEM((1, ROUTER_LANES), F32)],
        compiler_params=_params(("arbitrary",), 32),
        name="router",
    )(x, g, wr, br)


SC_CORES = 2
SC_SUBCORES = 16
SC_WORKERS = SC_CORES * SC_SUBCORES
SC_CHUNK = 64


def _sc_dispatch(h, dest0, dest1):
    n = h.shape[0]
    per_worker = n // SC_WORKERS
    assert per_worker % SC_CHUNK == 0
    mesh = plsc.VectorSubcoreMesh(core_axis_name="c", subcore_axis_name="s")

    @functools.partial(
        pl.kernel, mesh=mesh,
        out_type=jax.ShapeDtypeStruct((N_ROWS, *TOKEN_TILE), F32),
        scratch_types=[pltpu.VMEM((SC_CHUNK,), jnp.int32), pltpu.VMEM((SC_CHUNK,), jnp.int32),
                       pltpu.VMEM((SC_CHUNK, *TOKEN_TILE), F32), pltpu.SemaphoreType.DMA],
        name="moe_dispatch_sc")
    def scatter(h_hbm, d0_hbm, d1_hbm, xs_hbm, idx0, idx1, rows, sem):
        worker = lax.axis_index("s") * SC_CORES + lax.axis_index("c")
        base = worker * per_worker
        for c in range(per_worker // SC_CHUNK):
            t0 = base + c * SC_CHUNK
            pltpu.sync_copy(d0_hbm.at[pl.ds(t0, SC_CHUNK)], idx0)
            pltpu.sync_copy(d1_hbm.at[pl.ds(t0, SC_CHUNK)], idx1)
            pltpu.sync_copy(h_hbm.at[pl.ds(t0, SC_CHUNK)], rows)
            first = pltpu.async_copy(rows, xs_hbm.at[idx0], sem)
            second = pltpu.async_copy(rows, xs_hbm.at[idx1], sem)
            first.wait()
            second.wait()

    return scatter(h, dest0, dest1)


SC_GATHER_CHUNK = 32


def _sc_gather(y, dest0, dest1):
    n = dest0.shape[0]
    per_worker = n // SC_WORKERS
    assert per_worker % SC_GATHER_CHUNK == 0
    mesh = plsc.VectorSubcoreMesh(core_axis_name="c", subcore_axis_name="s")
    buf = pltpu.VMEM((SC_GATHER_CHUNK, *TOKEN_TILE), F32)
    idx = pltpu.VMEM((SC_GATHER_CHUNK,), jnp.int32)

    @functools.partial(
        pl.kernel, mesh=mesh,
        out_type=jax.ShapeDtypeStruct((TOP_K, n, *TOKEN_TILE), F32),
        scratch_types=[idx, idx, buf, buf, pltpu.SemaphoreType.DMA, pltpu.SemaphoreType.DMA],
        name="moe_gather_sc")
    def gather(y_hbm, d0_hbm, d1_hbm, out_hbm, idx0, idx1, rows0, rows1, sem0, sem1):
        worker = lax.axis_index("s") * SC_CORES + lax.axis_index("c")
        base = worker * per_worker
        for c in range(per_worker // SC_GATHER_CHUNK):
            t0 = base + c * SC_GATHER_CHUNK
            pltpu.sync_copy(d0_hbm.at[pl.ds(t0, SC_GATHER_CHUNK)], idx0)
            pltpu.sync_copy(d1_hbm.at[pl.ds(t0, SC_GATHER_CHUNK)], idx1)
            first = pltpu.async_copy(y_hbm.at[idx0], rows0, sem0)
            second = pltpu.async_copy(y_hbm.at[idx1], rows1, sem1)
            first.wait()
            pltpu.sync_copy(rows0, out_hbm.at[0, pl.ds(t0, SC_GATHER_CHUNK)])
            second.wait()
            pltpu.sync_copy(rows1, out_hbm.at[1, pl.ds(t0, SC_GATHER_CHUNK)])

    return gather(y, dest0, dest1)


N_STAGE = 3
WEIGHT_DMA_PRIORITY = 1
X_SLOTS = 4
Y_SLOTS = 4


def _expert_body(layer, blk_expert, blk_rows, next_expert, n_used, xs_hbm, wg_hbm, wu_hbm, wd_hbm,
                 y_hbm, xbuf, ybuf, sg, su, sd, wg, wu, wd, slot_ref, wsem, xsem, ysem):
    i = pl.program_id(0)
    e = blk_expert[i]
    nu = n_used[0]

    def x_copy(blk, slot):
        return pltpu.make_async_copy(xs_hbm.at[pl.ds(blk * ROW_BLOCK, ROW_BLOCK)], xbuf.at[slot],
                                     xsem.at[slot])

    def y_copy(blk, slot):
        return pltpu.make_async_copy(ybuf.at[slot], y_hbm.at[pl.ds(blk * ROW_BLOCK, ROW_BLOCK)],
                                     ysem.at[slot])

    def fetch(expert, slot):
        return (pltpu.make_async_copy(wg_hbm.at[layer, expert], sg.at[slot], wsem.at[slot, 0]),
                pltpu.make_async_copy(wu_hbm.at[layer, expert], su.at[slot], wsem.at[slot, 1]),
                pltpu.make_async_copy(wd_hbm.at[layer, expert], sd.at[slot], wsem.at[slot, 2]))

    def start_fetch(expert, slot):
        for c in fetch(expert, slot):
            c.start(priority=WEIGHT_DMA_PRIORITY)

    nxt = next_expert[e]
    nxt2 = next_expert[nxt]

    @pl.when(jnp.logical_and(i == 0, nu > 0))
    def _():
        slot_ref[0] = 0
        start_fetch(e, 0)

        @pl.when(nxt != e)
        def _():
            start_fetch(nxt, 1)

        for b in range(X_SLOTS - 1):
            @pl.when(b < nu)
            def _():
                x_copy(b, b).start()

    used = i < nu
    first_of_expert = jnp.logical_or(i == 0, blk_expert[jnp.maximum(i - 1, 0)] != e)
    xslot = i % X_SLOTS
    yslot = i % Y_SLOTS

    @pl.when(i >= Y_SLOTS)
    def _():
        y_copy(i - Y_SLOTS, yslot).wait()

    @pl.when(jnp.logical_and(used, first_of_expert))
    def _():
        slot = slot_ref[0]

        @pl.when(nxt2 != nxt)
        def _():
            start_fetch(nxt2, (slot + N_STAGE - 1) % N_STAGE)

        for c in fetch(e, slot):
            c.wait()
        wg[...] = sg[slot].astype(BF16)
        wu[...] = su[slot].astype(BF16)
        wd[...] = sd[slot].astype(BF16)
        slot_ref[0] = (slot + 1) % N_STAGE

    @pl.when(used)
    def _():
        ahead = i + X_SLOTS - 1

        @pl.when(ahead < nu)
        def _():
            x_copy(ahead, ahead % X_SLOTS).start()

        x_copy(i, xslot).wait()
        row = lax.broadcasted_iota(jnp.int32, (ROW_BLOCK, 1), 0)
        x = jnp.where(row < blk_rows[i], xbuf[xslot].reshape(ROW_BLOCK, D_MODEL), 0.0).astype(BF16)
        a = _dot(x, wg[...])
        b = _dot(x, wu[...])
        hmid = (a * (1.0 / (1.0 + jnp.exp(-a)))) * b
        ybuf[yslot] = _dot(hmid.astype(BF16), wd[...]).reshape(ROW_BLOCK, *TOKEN_TILE)

    @pl.when(jnp.logical_not(used))
    def _():
        ybuf[yslot] = jnp.zeros((ROW_BLOCK, *TOKEN_TILE), F32)

    y_copy(i, yslot).start()

    @pl.when(i == N_BLK - 1)
    def _():
        for s in range(Y_SLOTS):
            y_copy(0, s).wait()


def _experts(layer, blk_expert, blk_rows, next_expert, n_used, xs, w_gate, w_up, w_down):
    assert N_BLK >= Y_SLOTS
    hbm = pl.BlockSpec(memory_space=pl.ANY)
    return pl.pallas_call(
        functools.partial(_expert_body, layer),
        grid_spec=pltpu.PrefetchScalarGridSpec(
            num_scalar_prefetch=4,
            grid=(N_BLK,),
            in_specs=[hbm, hbm, hbm, hbm],
            out_specs=hbm,
            scratch_shapes=[
                pltpu.VMEM((X_SLOTS, ROW_BLOCK, *TOKEN_TILE), F32),
                pltpu.VMEM((Y_SLOTS, ROW_BLOCK, *TOKEN_TILE), F32),
                pltpu.VMEM((N_STAGE, D_MODEL, D_EXPERT), F32),
                pltpu.VMEM((N_STAGE, D_MODEL, D_EXPERT), F32),
                pltpu.VMEM((N_STAGE, D_EXPERT, D_MODEL), F32),
                pltpu.VMEM((D_MODEL, D_EXPERT), BF16),
                pltpu.VMEM((D_MODEL, D_EXPERT), BF16),
                pltpu.VMEM((D_EXPERT, D_MODEL), BF16),
                pltpu.SMEM((1,), jnp.int32),
                pltpu.SemaphoreType.DMA((N_STAGE, 3)),
                pltpu.SemaphoreType.DMA((X_SLOTS,)),
                pltpu.SemaphoreType.DMA((Y_SLOTS,)),
            ],
        ),
        out_shape=jax.ShapeDtypeStruct((N_ROWS, *TOKEN_TILE), F32),
        compiler_params=_params(("arbitrary",), 40),
        name="moe_experts",
    )(blk_expert, blk_rows, next_expert, n_used, xs, w_gate, w_up, w_down)


CB_TC = 512


def _combine_body(final, y0_ref, y1_ref, x_ref, rg_ref, fg_ref, out_ref):
    rg = rg_ref[...]
    y0 = y0_ref[...].reshape(CB_TC, D_MODEL)
    y1 = y1_ref[...].reshape(CB_TC, D_MODEL)
    out = x_ref[...] + rg[:, 0:1] * y0 + rg[:, 1:2] * y1
    out_ref[...] = _rms(out, fg_ref[...]) if final else out


def _combine(y2, x, rg, final_g, final):
    n = x.shape[0]
    return pl.pallas_call(
        functools.partial(_combine_body, final),
        grid=(n // CB_TC,),
        in_specs=[
            pl.BlockSpec((None, CB_TC, *TOKEN_TILE), lambda i: (0, i, 0, 0)),
            pl.BlockSpec((None, CB_TC, *TOKEN_TILE), lambda i: (1, i, 0, 0)),
            pl.BlockSpec((CB_TC, D_MODEL), lambda i: (i, 0)),
            pl.BlockSpec((CB_TC, ROUTER_LANES), lambda i: (i, 0)),
            pl.BlockSpec((1, D_MODEL), lambda i: (0, 0)),
        ],
        out_specs=pl.BlockSpec((CB_TC, D_MODEL), lambda i: (i, 0)),
        out_shape=jax.ShapeDtypeStruct((n, D_MODEL), F32),
        compiler_params=_params(("arbitrary",), 32),
        name="moe_combine",
    )(y2, y2, x, rg, final_g)


def _moe(layer, x, norm_g, group_w, group_b, exp_w, exp_b, w_gate, w_up, w_down, final_g, final):
    pad_w = ROUTER_LANES - N_GROUPS - N_EXPERTS
    wr = jnp.concatenate([group_w, exp_w, jnp.zeros((D_MODEL, pad_w), F32)], axis=1)
    br = jnp.concatenate([group_b, exp_b, jnp.zeros((pad_w,), F32)])[None, :]
    h, ri, rg, cnt = _router(x, norm_g[None, :], wr, br)

    experts = jnp.arange(N_EXPERTS, dtype=jnp.int32)
    counts = cnt[0, :N_EXPERTS].astype(jnp.int32)
    pcounts = (counts + ROW_BLOCK - 1) // ROW_BLOCK * ROW_BLOCK
    pends = jnp.cumsum(pcounts)
    pstarts = pends - pcounts
    dest0, dest1 = (
        (jnp.sum(jnp.where(ri[:, s, None] == experts, pstarts, 0), axis=-1) + ri[:, TOP_K + s]).astype(jnp.int32)
        for s in range(TOP_K))
    blk_row0 = jnp.arange(N_BLK, dtype=jnp.int32) * ROW_BLOCK
    blk_expert = jnp.minimum(jnp.sum(pends[None, :] <= blk_row0[:, None], axis=1),
                             N_EXPERTS - 1).astype(jnp.int32)
    n_used = (pends[-1:] // ROW_BLOCK).astype(jnp.int32)
    later_used = jnp.logical_and(experts[None, :] > experts[:, None], (pcounts > 0)[None, :])
    next_expert = jnp.min(jnp.where(later_used, experts[None, :], N_EXPERTS), axis=1)
    next_expert = jnp.where(next_expert == N_EXPERTS, experts, next_expert).astype(jnp.int32)

    owner = blk_expert[:, None] == experts[None, :]
    blk_rows = jnp.sum(jnp.where(owner, counts + pstarts, 0), axis=1) - blk_row0
    blk_rows = jnp.where(blk_row0 < pends[-1], jnp.clip(blk_rows, 0, ROW_BLOCK), 0).astype(jnp.int32)

    xs = _sc_dispatch(h, dest0, dest1)
    y = _experts(layer, blk_expert, blk_rows, next_expert, n_used, xs, w_gate, w_up, w_down)
    return _combine(_sc_gather(y, dest0, dest1), x, rg, final_g, final)


def kernel(x, norm_mix, norm_ffn, norm_final, ab_w_in, ab_w_out, pool_w, pool_scale, gm_w_in, gm_norm, gm_ws, gm_bs, gm_w_out, router_group_w, router_group_b, router_expert_w, router_expert_b, moe_w_gate, moe_w_up, moe_w_down):
    assert x.shape == (BATCH, SEQ, D_MODEL) and x.dtype == F32
    xf = x.reshape(N_TOK, D_MODEL)
    for layer in range(DEPTH):
        i = layer // 2
        if layer % 2 == 0:
            zp, qkv = _ab_in(xf, norm_mix[layer][None, :], ab_w_in[i].astype(BF16))
            o = _attention(qkv.reshape(BATCH, SEQ, 3 * ATTN_WIDTH))
            xf = _ab_out(zp.reshape(BATCH, SEQ, POOL_WIDTH), o, xf.reshape(BATCH, SEQ, D_MODEL),
                         pool_w[i].astype(BF16), pool_scale[i][None, :],
                         ab_w_out[i].astype(BF16)).reshape(N_TOK, D_MODEL)
        else:
            bs_bcast = jnp.broadcast_to(gm_bs[i][:, :, None], (GM_GROUPS, GM_CHUNK, GM_CHUNK))
            xf = _gmlp(xf, norm_mix[layer][None, :], gm_w_in[i].astype(BF16), gm_norm[i][None, :],
                       gm_ws[i], bs_bcast, gm_w_out[i].astype(BF16))
        xf = _moe(layer, xf, norm_ffn[layer], router_group_w[layer], router_group_b[layer],
                  router_expert_w[layer], router_expert_b[layer], moe_w_gate, moe_w_up, moe_w_down,
                  norm_final[None, :], layer == DEPTH - 1)
    return xf.reshape(BATCH, SEQ, D_MODEL)
```

```python
import functools

import jax
import jax.numpy as jnp
from jax import lax
from jax.experimental import pallas as pl
from jax.experimental.pallas import tpu as pltpu
from jax.experimental.pallas import tpu_sc as plsc

D_MODEL = 1024
BATCH = 8
SEQ = 2048
DEPTH = 4
N_TOK = BATCH * SEQ

POOL_WINDOWS = (2, 4, 8, 16)
POOL_CH = 128
POOL_WIDTH = 512
POOL_HALO = 16
HEAD_DIM = 64
ATTN_WIDTH = 512
HEADS_PER_STEP = 2
ATTN_BLOCK = 128
GM_CHUNK = 128
GM_GROUPS = 8
GM_WIDTH = 1024
N_GROUPS = 8
N_EXPERTS = 64
TOP_K = 2
D_EXPERT = 512
ROW_BLOCK = 128
N_ROWS = (N_TOK * TOP_K + N_EXPERTS * (ROW_BLOCK - 1) + ROW_BLOCK - 1) // ROW_BLOCK * ROW_BLOCK
N_BLK = N_ROWS // ROW_BLOCK
EPS = 1e-6
NEG = -1e30

LANES = 128
SUBLANES = 8
TOKEN_TILE = (SUBLANES, LANES)
assert SUBLANES * LANES == D_MODEL
V7X_VMEM_BYTES = 64 * 1024 * 1024
MIB = 1024 * 1024

F32 = jnp.float32
BF16 = jnp.bfloat16


def _params(semantics, vmem_mib):
    assert vmem_mib * MIB < V7X_VMEM_BYTES
    return pltpu.CompilerParams(dimension_semantics=semantics, vmem_limit_bytes=vmem_mib * MIB)


def _rms(x, g):
    return x * lax.rsqrt(jnp.mean(x * x, axis=-1, keepdims=True) + EPS) * g


def _dot(a, b):
    return jnp.dot(a, b, preferred_element_type=F32)


def _dot_nt(a, b):
    return lax.dot_general(a, b, (((1,), (1,)), ((), ())), preferred_element_type=F32)


AB_TM = 512


def _moe_specs(rows):
    return [
        pl.BlockSpec((None, rows, *TOKEN_TILE), lambda i: (0, i, 0, 0)),
        pl.BlockSpec((None, rows, *TOKEN_TILE), lambda i: (1, i, 0, 0)),
        pl.BlockSpec((rows, ROUTER_LANES), lambda i: (i, 0)),
    ]


def _add_moe(x, y0_ref, y1_ref, rg_ref):
    rows = x.shape[0]
    rg = rg_ref[...]
    return (x + rg[:, 0:1] * y0_ref[...].reshape(rows, D_MODEL)
            + rg[:, 1:2] * y1_ref[...].reshape(rows, D_MODEL))


def _ab_in_body(has_moe, *refs):
    if has_moe:
        y0_ref, y1_ref, rg_ref, x_ref, g_ref, w_ref, xnew_ref, zp_ref, qkv_ref = refs
        x = _add_moe(x_ref[...], y0_ref, y1_ref, rg_ref)
        xnew_ref[...] = x
    else:
        x_ref, g_ref, w_ref, zp_ref, qkv_ref = refs
        x = x_ref[...]
    h = _rms(x, g_ref[...]).astype(BF16)
    z = _dot(h, w_ref[...])
    zp_ref[...] = z[:, :POOL_WIDTH]
    q = z[:, POOL_WIDTH:POOL_WIDTH + ATTN_WIDTH] * (HEAD_DIM ** -0.5)
    qkv_ref[...] = jnp.concatenate([q, z[:, POOL_WIDTH + ATTN_WIDTH:]], axis=1).astype(BF16)


def _ab_in(x, moe, g, w_bf16):
    n = x.shape[0]
    has_moe = moe is not None
    row_spec = pl.BlockSpec((AB_TM, D_MODEL), lambda i: (i, 0))
    outs = pl.pallas_call(
        functools.partial(_ab_in_body, has_moe),
        grid=(n // AB_TM,),
        in_specs=(_moe_specs(AB_TM) if has_moe else []) + [
            row_spec,
            pl.BlockSpec((1, D_MODEL), lambda i: (0, 0)),
            pl.BlockSpec((D_MODEL, 4 * ATTN_WIDTH), lambda i: (0, 0)),
        ],
        out_specs=([row_spec] if has_moe else []) + [
            pl.BlockSpec((AB_TM, POOL_WIDTH), lambda i: (i, 0)),
            pl.BlockSpec((AB_TM, 3 * ATTN_WIDTH), lambda i: (i, 0)),
        ],
        out_shape=([jax.ShapeDtypeStruct((n, D_MODEL), F32)] if has_moe else []) + [
            jax.ShapeDtypeStruct((n, POOL_WIDTH), F32),
            jax.ShapeDtypeStruct((n, 3 * ATTN_WIDTH), BF16),
        ],
        compiler_params=_params(("arbitrary",), 48),
        name="ab_in",
    )(*((moe[0], moe[0], moe[1]) if has_moe else ()), x, g, w_bf16)
    return tuple(outs) if has_moe else (x, *outs)


ATTN_UNROLL = 4


def _attn_body(q_ref, k_ref, v_ref, o_ref, qf, kf, vf, o1, o4, o16, l1, l4, l16):
    qf[...] = q_ref[...].astype(F32)
    kf[...] = k_ref[...].astype(F32)
    vf[...] = v_ref[...].astype(F32)

    key = lax.broadcasted_iota(jnp.int32, (ATTN_BLOCK, HEADS_PER_STEP * ATTN_BLOCK), 0)
    qry = lax.broadcasted_iota(jnp.int32, (ATTN_BLOCK, HEADS_PER_STEP * ATTN_BLOCK), 1) % ATTN_BLOCK
    cur_ok = key <= qry
    prev_ok = key >= qry
    head0_lane = lax.broadcasted_iota(jnp.int32, (ATTN_BLOCK, LANES), 1) < HEAD_DIM

    def rows(ref, start, d):
        if d == 1:
            return ref[pl.ds(start, ATTN_BLOCK), :]
        return ref[pl.ds(start, ATTN_BLOCK, stride=d), :]

    def put(ref, start, d, val):
        if d == 1:
            ref[pl.ds(start, ATTN_BLOCK), :] = val
        else:
            ref[pl.ds(start, ATTN_BLOCK, stride=d), :] = val

    for d, o_acc, l_acc in ((1, o1, l1), (4, o4, l4), (16, o16, l16)):
        nb = SEQ // d // ATTN_BLOCK
        has_prev = nb > 1

        n_blocks = SEQ // ATTN_BLOCK
        halves = [slice(hh * ATTN_BLOCK, (hh + 1) * ATTN_BLOCK) for hh in range(HEADS_PER_STEP)]
        chans = [slice(hh * HEAD_DIM, (hh + 1) * HEAD_DIM) for hh in range(HEADS_PER_STEP)]

        def block_start(idx, d=d, nb=nb):
            return idx // nb + (idx % nb) * (ATTN_BLOCK * d)

        def scores(idx, k_prev, d=d, has_prev=has_prev):
            start = block_start(idx)
            qv = rows(qf, start, d)
            q2 = jnp.concatenate([jnp.where(head0_lane, qv, 0.0), jnp.where(head0_lane, 0.0, qv)],
                                 axis=0).astype(BF16)
            kc = rows(kf, start, d).astype(BF16)
            keys = jnp.concatenate([k_prev, kc], axis=0) if has_prev else kc
            return _dot_nt(keys, q2), kc

        def finish(pv, den, lse, start, d=d, o_acc=o_acc, l_acc=l_acc):
            out_t = jnp.concatenate([pv[c, h] / den[:, h] for c, h in zip(chans, halves)], axis=0)
            lse_t = jnp.concatenate([jnp.broadcast_to(lse[:, h], (HEAD_DIM, ATTN_BLOCK)) for h in halves],
                                    axis=0)
            put(o_acc, start, d, out_t.T)
            put(l_acc, start, d, lse_t.T)

        def block(idx, carry, d=d, nb=nb, has_prev=has_prev):
            s_raw, k_cur, v_prev_t, pv_last, den_last, lse_last, start_last = carry
            s_next, k_next = scores(jnp.minimum(idx + 1, n_blocks - 1), k_cur)
            finish(pv_last, den_last, lse_last, start_last)
            start = block_start(idx)
            vc_t = rows(vf, start, d).T.astype(BF16)
            if has_prev:
                vals_t = jnp.concatenate([v_prev_t, vc_t], axis=1)
                ok = jnp.concatenate([jnp.logical_and(prev_ok, idx % nb > 0), cur_ok], axis=0)
            else:
                vals_t, ok = vc_t, cur_ok
            s = jnp.where(ok, s_raw, NEG)
            m = jnp.max(s, axis=0, keepdims=True)
            p = jnp.exp(s - m)
            den = jnp.sum(p, axis=0, keepdims=True)
            pv = _dot(vals_t, p.astype(BF16))
            return s_next, k_next, vc_t, pv, den, m + jnp.log(den), start

        k_none = jnp.zeros((ATTN_BLOCK, LANES), BF16)
        s_first, k_first = scores(0, k_none)
        width = HEADS_PER_STEP * ATTN_BLOCK
        init = (s_first, k_first, jnp.zeros((LANES, ATTN_BLOCK), BF16), jnp.zeros((LANES, width), F32),
                jnp.ones((1, width), F32), jnp.zeros((1, width), F32), jnp.int32(0))
        last = lax.fori_loop(0, n_blocks, block, init, unroll=ATTN_UNROLL)
        finish(*last[3:])

    def merge(c, carry):
        sl = pl.ds(pl.multiple_of(c * ATTN_BLOCK, ATTN_BLOCK), ATTN_BLOCK)
        la, lb, lc = l1[sl, :], l4[sl, :], l16[sl, :]
        mx = jnp.maximum(jnp.maximum(la, lb), lc)
        wa, wb, wc = jnp.exp(la - mx), jnp.exp(lb - mx), jnp.exp(lc - mx)
        num = wa * o1[sl, :] + wb * o4[sl, :] + wc * o16[sl, :]
        o_ref[sl, :] = (num / (wa + wb + wc)).astype(o_ref.dtype)
        return carry

    lax.fori_loop(0, SEQ // ATTN_BLOCK, merge, 0)


def _attention(qkv):
    n_hp = ATTN_WIDTH // LANES
    blk = (None, SEQ, LANES)
    return pl.pallas_call(
        _attn_body,
        grid=(qkv.shape[0], n_hp),
        in_specs=[
            pl.BlockSpec(blk, lambda b, hp: (b, 0, hp)),
            pl.BlockSpec(blk, lambda b, hp: (b, 0, n_hp + hp)),
            pl.BlockSpec(blk, lambda b, hp: (b, 0, 2 * n_hp + hp)),
        ],
        out_specs=pl.BlockSpec(blk, lambda b, hp: (b, 0, hp)),
        out_shape=jax.ShapeDtypeStruct((qkv.shape[0], SEQ, ATTN_WIDTH), BF16),
        scratch_shapes=[pltpu.VMEM((SEQ, LANES), F32) for _ in range(9)],
        compiler_params=_params(("arbitrary", "arbitrary"), 32),
        name="dilated_attn",
    )(qkv, qkv, qkv)


AO_TS = 512


def _ab_out_body(zp_ref, halo_ref, o_ref, x_ref, pw_ref, ps_ref, w_ref, out_ref, zz):
    i = pl.program_id(1)
    zz[0:POOL_HALO, :] = jnp.where(i > 0, halo_ref[...], 0.0)
    zz[POOL_HALO:, :] = zp_ref[...]
    pos = i * AO_TS + lax.broadcasted_iota(jnp.int32, (AO_TS, 1), 0)
    parts = []
    for g, w in enumerate(POOL_WINDOWS):
        cols = slice(g * POOL_CH, (g + 1) * POOL_CH)
        acc = zz[pl.ds(POOL_HALO, AO_TS), cols]
        for j in range(1, w):
            acc = acc + zz[pl.ds(POOL_HALO - j, AO_TS), cols]
        count = jnp.minimum(pos + 1, w).astype(F32)
        pooled = acc / count - zp_ref[:, cols]
        parts.append(_dot(pooled.astype(BF16), pw_ref[g]))
    a = jnp.concatenate(parts, axis=1) * ps_ref[...]
    y = jnp.concatenate([a.astype(BF16), o_ref[...]], axis=1)
    out_ref[...] = x_ref[...] + _dot(y, w_ref[...])


def _ab_out(zp, o, x, pool_w_bf16, pool_scale, w_out_bf16):
    halo_per_tile = AO_TS // POOL_HALO
    return pl.pallas_call(
        _ab_out_body,
        grid=(BATCH, SEQ // AO_TS),
        in_specs=[
            pl.BlockSpec((None, AO_TS, POOL_WIDTH), lambda b, i: (b, i, 0)),
            pl.BlockSpec((None, POOL_HALO, POOL_WIDTH),
                         lambda b, i: (b, jnp.maximum(i * halo_per_tile - 1, 0), 0)),
            pl.BlockSpec((None, AO_TS, ATTN_WIDTH), lambda b, i: (b, i, 0)),
            pl.BlockSpec((None, AO_TS, D_MODEL), lambda b, i: (b, i, 0)),
            pl.BlockSpec((len(POOL_WINDOWS), POOL_CH, POOL_CH), lambda b, i: (0, 0, 0)),
            pl.BlockSpec((1, POOL_WIDTH), lambda b, i: (0, 0)),
            pl.BlockSpec((POOL_WIDTH + ATTN_WIDTH, D_MODEL), lambda b, i: (0, 0)),
        ],
        out_specs=pl.BlockSpec((None, AO_TS, D_MODEL), lambda b, i: (b, i, 0)),
        out_shape=jax.ShapeDtypeStruct((BATCH, SEQ, D_MODEL), F32),
        scratch_shapes=[pltpu.VMEM((AO_TS + POOL_HALO, POOL_WIDTH), F32)],
        compiler_params=_params(("arbitrary", "arbitrary"), 32),
        name="pool_ab_out",
    )(zp, zp, o, x, pool_w_bf16, pool_scale, w_out_bf16)


GM_TS = 256


def _gelu_tanh(x):
    return 0.5 * x * (1.0 + jnp.tanh(0.7978845608028654 * (x + 0.044715 * (x * x * x))))


def _gmlp_body(y0_ref, y1_ref, rg_ref, x_ref, g_ref, win_ref, gn_ref, ws_ref, bs_ref, wout_ref, out_ref):
    x = _add_moe(x_ref[...], y0_ref, y1_ref, rg_ref)
    h = _rms(x, g_ref[...]).astype(BF16)
    z = _gelu_tanh(_dot(h, win_ref[...]))
    u = z[:, :GM_WIDTH]
    v = z[:, GM_WIDTH:]
    vc = v - jnp.mean(v, axis=-1, keepdims=True)
    vn = (vc * lax.rsqrt(jnp.mean(vc * vc, axis=-1, keepdims=True) + EPS) * gn_ref[...]).astype(BF16)
    ri = lax.broadcasted_iota(jnp.int32, (GM_CHUNK, GM_CHUNK), 0)
    ci = lax.broadcasted_iota(jnp.int32, (GM_CHUNK, GM_CHUNK), 1)
    causal = ci <= ri
    rows = []
    for c in range(GM_TS // GM_CHUNK):
        rsl = slice(c * GM_CHUNK, (c + 1) * GM_CHUNK)
        cols = []
        for g in range(GM_GROUPS):
            csl = slice(g * (GM_WIDTH // GM_GROUPS), (g + 1) * (GM_WIDTH // GM_GROUPS))
            ws_c = jnp.where(causal, ws_ref[g], 0.0).astype(BF16)
            sv = _dot(ws_c, vn[rsl, csl]) + bs_ref[g]
            cols.append(u[rsl, csl] * sv)
        rows.append(jnp.concatenate(cols, axis=1))
    y = jnp.concatenate(rows, axis=0).astype(BF16)
    out_ref[...] = x + _dot(y, wout_ref[...])


def _gmlp(x, moe, g, w_in_bf16, gm_norm, ws, bs_bcast, w_out_bf16):
    n = x.shape[0]
    const2 = lambda i: (0, 0)
    const3 = lambda i: (0, 0, 0)
    return pl.pallas_call(
        _gmlp_body,
        grid=(n // GM_TS,),
        in_specs=_moe_specs(GM_TS) + [
            pl.BlockSpec((GM_TS, D_MODEL), lambda i: (i, 0)),
            pl.BlockSpec((1, D_MODEL), const2),
            pl.BlockSpec((D_MODEL, 2 * GM_WIDTH), const2),
            pl.BlockSpec((1, GM_WIDTH), const2),
            pl.BlockSpec((GM_GROUPS, GM_CHUNK, GM_CHUNK), const3),
            pl.BlockSpec((GM_GROUPS, GM_CHUNK, GM_CHUNK), const3),
            pl.BlockSpec((GM_WIDTH, D_MODEL), const2),
        ],
        out_specs=pl.BlockSpec((GM_TS, D_MODEL), lambda i: (i, 0)),
        out_shape=jax.ShapeDtypeStruct((n, D_MODEL), F32),
        compiler_params=_params(("arbitrary",), 40),
        name="gmlp",
    )(moe[0], moe[0], moe[1], x, g, w_in_bf16, gm_norm, ws, bs_bcast, w_out_bf16)


RT_TN = 512
ROUTER_LANES = LANES
EXPERT_LANE0 = N_GROUPS


def _router_body(x_ref, g_ref, wr_ref, br_ref, h_ref, ri_ref, rg_ref, cnt_ref, tri, running):
    step = pl.program_id(0)

    @pl.when(step == 0)
    def _():
        r = lax.broadcasted_iota(jnp.int32, (RT_TN, RT_TN), 0)
        c = lax.broadcasted_iota(jnp.int32, (RT_TN, RT_TN), 1)
        tri[...] = jnp.where(c < r, 1.0, 0.0).astype(BF16)
        running[...] = jnp.zeros_like(running)

    h = _rms(x_ref[...], g_ref[...])
    h_ref[...] = h.reshape(RT_TN, *TOKEN_TILE)
    h_hi = h.astype(BF16)
    h_lo = (h - h_hi.astype(F32)).astype(BF16)
    logits = (_dot(h_hi, wr_ref[0]) + (_dot(h_hi, wr_ref[1]) + _dot(h_lo, wr_ref[0]))) + br_ref[...]
    lane = lax.broadcasted_iota(jnp.int32, (RT_TN, ROUTER_LANES), 1)

    is_group = lane < N_GROUPS
    lg = jnp.where(is_group, logits, NEG)
    mg = jnp.max(lg, axis=1, keepdims=True)
    p_top = 1.0 / jnp.sum(jnp.where(is_group, jnp.exp(logits - mg), 0.0), axis=1, keepdims=True)
    g_idx = jnp.min(jnp.where(lg == mg, lane, ROUTER_LANES), axis=1, keepdims=True)

    in_group = jnp.logical_and(
        jnp.logical_and(lane >= EXPERT_LANE0, lane < EXPERT_LANE0 + N_EXPERTS),
        ((lane - EXPERT_LANE0) >> 3) == g_idx)
    le = jnp.where(in_group, logits, NEG)
    t1 = jnp.max(le, axis=1, keepdims=True)
    i1 = jnp.min(jnp.where(le == t1, lane, ROUTER_LANES), axis=1, keepdims=True)
    le2 = jnp.where(lane == i1, NEG, le)
    t2 = jnp.max(le2, axis=1, keepdims=True)
    i2 = jnp.min(jnp.where(le2 == t2, lane, ROUTER_LANES), axis=1, keepdims=True)
    e0 = i1 - EXPERT_LANE0
    e1 = i2 - EXPERT_LANE0
    ex = jnp.exp(t2 - t1)
    w0 = 1.0 / (1.0 + ex)
    g0 = p_top * w0
    g1 = p_top * (ex * w0)

    oh0 = lane == e0
    oh1 = lane == e1
    cnt = jnp.where(jnp.logical_or(oh0, oh1), 1.0, 0.0)
    base = running[...] + _dot(tri[...], cnt.astype(BF16))
    rank0 = jnp.sum(jnp.where(oh0, base, 0.0), axis=1, keepdims=True).astype(jnp.int32)
    rank1 = jnp.sum(jnp.where(oh1, base, 0.0), axis=1, keepdims=True).astype(jnp.int32)
    running[...] = running[...] + jnp.sum(cnt, axis=0, keepdims=True)
    cnt_ref[...] = running[...]

    ri_ref[...] = jnp.where(lane == 0, e0, jnp.where(lane == 1, e1,
                            jnp.where(lane == 2, rank0, jnp.where(lane == 3, rank1, 0))))
    rg_ref[...] = jnp.where(lane == 0, g0, jnp.where(lane == 1, g1, 0.0))


def _router(x, g, wr, br):
    n = x.shape[0]
    const2 = lambda i: (0, 0)
    return pl.pallas_call(
        _router_body,
        grid=(n // RT_TN,),
        in_specs=[
            pl.BlockSpec((RT_TN, D_MODEL), lambda i: (i, 0)),
            pl.BlockSpec((1, D_MODEL), const2),
            pl.BlockSpec((2, D_MODEL, ROUTER_LANES), lambda i: (0, 0, 0)),
            pl.BlockSpec((1, ROUTER_LANES), const2),
        ],
        out_specs=[
            pl.BlockSpec((RT_TN, *TOKEN_TILE), lambda i: (i, 0, 0)),
            pl.BlockSpec((RT_TN, ROUTER_LANES), lambda i: (i, 0)),
            pl.BlockSpec((RT_TN, ROUTER_LANES), lambda i: (i, 0)),
            pl.BlockSpec((1, ROUTER_LANES), const2),
        ],
        out_shape=[
            jax.ShapeDtypeStruct((n, *TOKEN_TILE), F32),
            jax.ShapeDtypeStruct((n, ROUTER_LANES), jnp.int32),
            jax.ShapeDtypeStruct((n, ROUTER_LANES), F32),
            jax.ShapeDtypeStruct((1, ROUTER_LANES), F32),
        ],
        scratch_shapes=[pltpu.VMEM((RT_TN, RT_TN), BF16), pltpu.VMEM((1, ROUTER_LANES), F32)],
        compiler_params=_params(("arbitrary",), 32),
        name="router",
    )(x, g, wr, br)


SC_CORES = 2
SC_SUBCORES = 16
SC_WORKERS = SC_CORES * SC_SUBCORES
SC_CHUNK = 64


def _sc_dispatch(h, dest0, dest1):
    n = h.shape[0]
    per_worker = n // SC_WORKERS
    assert per_worker % SC_CHUNK == 0
    mesh = plsc.VectorSubcoreMesh(core_axis_name="c", subcore_axis_name="s")

    @functools.partial(
        pl.kernel, mesh=mesh,
        out_type=jax.ShapeDtypeStruct((N_ROWS, *TOKEN_TILE), F32),
        scratch_types=[pltpu.VMEM((SC_CHUNK,), jnp.int32), pltpu.VMEM((SC_CHUNK,), jnp.int32),
                       pltpu.VMEM((SC_CHUNK, *TOKEN_TILE), F32), pltpu.SemaphoreType.DMA],
        name="moe_dispatch_sc")
    def scatter(h_hbm, d0_hbm, d1_hbm, xs_hbm, idx0, idx1, rows, sem):
        worker = lax.axis_index("s") * SC_CORES + lax.axis_index("c")
        base = worker * per_worker
        for c in range(per_worker // SC_CHUNK):
            t0 = base + c * SC_CHUNK
            pltpu.sync_copy(d0_hbm.at[pl.ds(t0, SC_CHUNK)], idx0)
            pltpu.sync_copy(d1_hbm.at[pl.ds(t0, SC_CHUNK)], idx1)
            pltpu.sync_copy(h_hbm.at[pl.ds(t0, SC_CHUNK)], rows)
            first = pltpu.async_copy(rows, xs_hbm.at[idx0], sem)
            second = pltpu.async_copy(rows, xs_hbm.at[idx1], sem)
            first.wait()
            second.wait()

    return scatter(h, dest0, dest1)


SC_GATHER_CHUNK = 32


def _sc_gather(y, dest0, dest1):
    n = dest0.shape[0]
    per_worker = n // SC_WORKERS
    assert per_worker % SC_GATHER_CHUNK == 0
    mesh = plsc.VectorSubcoreMesh(core_axis_name="c", subcore_axis_name="s")
    buf = pltpu.VMEM((SC_GATHER_CHUNK, *TOKEN_TILE), F32)
    idx = pltpu.VMEM((SC_GATHER_CHUNK,), jnp.int32)

    @functools.partial(
        pl.kernel, mesh=mesh,
        out_type=jax.ShapeDtypeStruct((TOP_K, n, *TOKEN_TILE), F32),
        scratch_types=[idx, idx, buf, buf, pltpu.SemaphoreType.DMA, pltpu.SemaphoreType.DMA],
        name="moe_gather_sc")
    def gather(y_hbm, d0_hbm, d1_hbm, out_hbm, idx0, idx1, rows0, rows1, sem0, sem1):
        worker = lax.axis_index("s") * SC_CORES + lax.axis_index("c")
        base = worker * per_worker
        for c in range(per_worker // SC_GATHER_CHUNK):
            t0 = base + c * SC_GATHER_CHUNK
            pltpu.sync_copy(d0_hbm.at[pl.ds(t0, SC_GATHER_CHUNK)], idx0)
            pltpu.sync_copy(d1_hbm.at[pl.ds(t0, SC_GATHER_CHUNK)], idx1)
            first = pltpu.async_copy(y_hbm.at[idx0], rows0, sem0)
            second = pltpu.async_copy(y_hbm.at[idx1], rows1, sem1)
            first.wait()
            pltpu.sync_copy(rows0, out_hbm.at[0, pl.ds(t0, SC_GATHER_CHUNK)])
            second.wait()
            pltpu.sync_copy(rows1, out_hbm.at[1, pl.ds(t0, SC_GATHER_CHUNK)])

    return gather(y, dest0, dest1)


N_STAGE = 3
WEIGHT_DMA_PRIORITY = 1
X_SLOTS = 4
Y_SLOTS = 4


def _expert_body(layer, blk_expert, blk_rows, next_expert, n_used, xs_hbm, wg_hbm, wu_hbm, wd_hbm,
                 y_hbm, xbuf, ybuf, sg, su, sd, wg, wu, wd, slot_ref, wsem, xsem, ysem):
    i = pl.program_id(0)
    e = blk_expert[i]
    nu = n_used[0]

    def x_copy(blk, slot):
        return pltpu.make_async_copy(xs_hbm.at[pl.ds(blk * ROW_BLOCK, ROW_BLOCK)], xbuf.at[slot],
                                     xsem.at[slot])

    def y_copy(blk, slot):
        return pltpu.make_async_copy(ybuf.at[slot], y_hbm.at[pl.ds(blk * ROW_BLOCK, ROW_BLOCK)],
                                     ysem.at[slot])

    def fetch(expert, slot):
        return (pltpu.make_async_copy(wg_hbm.at[layer, expert], sg.at[slot], wsem.at[slot, 0]),
                pltpu.make_async_copy(wu_hbm.at[layer, expert], su.at[slot], wsem.at[slot, 1]),
                pltpu.make_async_copy(wd_hbm.at[layer, expert], sd.at[slot], wsem.at[slot, 2]))

    def start_fetch(expert, slot):
        for c in fetch(expert, slot):
            c.start(priority=WEIGHT_DMA_PRIORITY)

    nxt = next_expert[e]
    nxt2 = next_expert[nxt]

    @pl.when(jnp.logical_and(i == 0, nu > 0))
    def _():
        slot_ref[0] = 0
        start_fetch(e, 0)

        @pl.when(nxt != e)
        def _():
            start_fetch(nxt, 1)

        for b in range(X_SLOTS - 1):
            @pl.when(b < nu)
            def _():
                x_copy(b, b).start()

    used = i < nu
    first_of_expert = jnp.logical_or(i == 0, blk_expert[jnp.maximum(i - 1, 0)] != e)
    xslot = i % X_SLOTS
    yslot = i % Y_SLOTS

    @pl.when(i >= Y_SLOTS)
    def _():
        y_copy(i - Y_SLOTS, yslot).wait()

    @pl.when(jnp.logical_and(used, first_of_expert))
    def _():
        slot = slot_ref[0]

        @pl.when(nxt2 != nxt)
        def _():
            start_fetch(nxt2, (slot + N_STAGE - 1) % N_STAGE)

        for c in fetch(e, slot):
            c.wait()
        wg[...] = sg[slot].astype(BF16)
        wu[...] = su[slot].astype(BF16)
        wd[...] = sd[slot].astype(BF16)
        slot_ref[0] = (slot + 1) % N_STAGE

    @pl.when(used)
    def _():
        ahead = i + X_SLOTS - 1

        @pl.when(ahead < nu)
        def _():
            x_copy(ahead, ahead % X_SLOTS).start()

        x_copy(i, xslot).wait()
        row = lax.broadcasted_iota(jnp.int32, (ROW_BLOCK, 1), 0)
        x = jnp.where(row < blk_rows[i], xbuf[xslot].reshape(ROW_BLOCK, D_MODEL), 0.0).astype(BF16)
        a = _dot(x, wg[...])
        b = _dot(x, wu[...])
        hmid = (a * (1.0 / (1.0 + jnp.exp(-a)))) * b
        ybuf[yslot] = _dot(hmid.astype(BF16), wd[...]).reshape(ROW_BLOCK, *TOKEN_TILE)

    @pl.when(jnp.logical_not(used))
    def _():
        ybuf[yslot] = jnp.zeros((ROW_BLOCK, *TOKEN_TILE), F32)

    y_copy(i, yslot).start()

    @pl.when(i == N_BLK - 1)
    def _():
        for s in range(Y_SLOTS):
            y_copy(0, s).wait()


def _experts(layer, blk_expert, blk_rows, next_expert, n_used, xs, w_gate, w_up, w_down):
    assert N_BLK >= Y_SLOTS
    hbm = pl.BlockSpec(memory_space=pl.ANY)
    return pl.pallas_call(
        functools.partial(_expert_body, layer),
        grid_spec=pltpu.PrefetchScalarGridSpec(
            num_scalar_prefetch=4,
            grid=(N_BLK,),
            in_specs=[hbm, hbm, hbm, hbm],
            out_specs=hbm,
            scratch_shapes=[
                pltpu.VMEM((X_SLOTS, ROW_BLOCK, *TOKEN_TILE), F32),
                pltpu.VMEM((Y_SLOTS, ROW_BLOCK, *TOKEN_TILE), F32),
                pltpu.VMEM((N_STAGE, D_MODEL, D_EXPERT), F32),
                pltpu.VMEM((N_STAGE, D_MODEL, D_EXPERT), F32),
                pltpu.VMEM((N_STAGE, D_EXPERT, D_MODEL), F32),
                pltpu.VMEM((D_MODEL, D_EXPERT), BF16),
                pltpu.VMEM((D_MODEL, D_EXPERT), BF16),
                pltpu.VMEM((D_EXPERT, D_MODEL), BF16),
                pltpu.SMEM((1,), jnp.int32),
                pltpu.SemaphoreType.DMA((N_STAGE, 3)),
                pltpu.SemaphoreType.DMA((X_SLOTS,)),
                pltpu.SemaphoreType.DMA((Y_SLOTS,)),
            ],
        ),
        out_shape=jax.ShapeDtypeStruct((N_ROWS, *TOKEN_TILE), F32),
        compiler_params=_params(("arbitrary",), 40),
        name="moe_experts",
    )(blk_expert, blk_rows, next_expert, n_used, xs, w_gate, w_up, w_down)


CB_TC = 512


def _combine_body(y0_ref, y1_ref, rg_ref, x_ref, fg_ref, out_ref):
    out_ref[...] = _rms(_add_moe(x_ref[...], y0_ref, y1_ref, rg_ref), fg_ref[...])


def _combine(y2, x, rg, final_g):
    n = x.shape[0]
    return pl.pallas_call(
        _combine_body,
        grid=(n // CB_TC,),
        in_specs=_moe_specs(CB_TC) + [
            pl.BlockSpec((CB_TC, D_MODEL), lambda i: (i, 0)),
            pl.BlockSpec((1, D_MODEL), lambda i: (0, 0)),
        ],
        out_specs=pl.BlockSpec((CB_TC, D_MODEL), lambda i: (i, 0)),
        out_shape=jax.ShapeDtypeStruct((n, D_MODEL), F32),
        compiler_params=_params(("arbitrary",), 32),
        name="moe_combine",
    )(y2, y2, rg, x, final_g)


def _moe(layer, x, norm_g, group_w, group_b, exp_w, exp_b, w_gate, w_up, w_down):
    pad_w = ROUTER_LANES - N_GROUPS - N_EXPERTS
    wr = jnp.concatenate([group_w, exp_w, jnp.zeros((D_MODEL, pad_w), F32)], axis=1)
    br = jnp.concatenate([group_b, exp_b, jnp.zeros((pad_w,), F32)])[None, :]
    wr_hi = wr.astype(BF16)
    wr_lo = (wr - wr_hi.astype(F32)).astype(BF16)
    h, ri, rg, cnt = _router(x, norm_g[None, :], jnp.stack([wr_hi, wr_lo]), br)

    experts = jnp.arange(N_EXPERTS, dtype=jnp.int32)
    counts = cnt[0, :N_EXPERTS].astype(jnp.int32)
    pcounts = (counts + ROW_BLOCK - 1) // ROW_BLOCK * ROW_BLOCK
    pends = jnp.cumsum(pcounts)
    pstarts = pends - pcounts
    dest0, dest1 = (
        (jnp.sum(jnp.where(ri[:, s, None] == experts, pstarts, 0), axis=-1) + ri[:, TOP_K + s]).astype(jnp.int32)
        for s in range(TOP_K))
    blk_row0 = jnp.arange(N_BLK, dtype=jnp.int32) * ROW_BLOCK
    blk_expert = jnp.minimum(jnp.sum(pends[None, :] <= blk_row0[:, None], axis=1),
                             N_EXPERTS - 1).astype(jnp.int32)
    n_used = (pends[-1:] // ROW_BLOCK).astype(jnp.int32)
    later_used = jnp.logical_and(experts[None, :] > experts[:, None], (pcounts > 0)[None, :])
    next_expert = jnp.min(jnp.where(later_used, experts[None, :], N_EXPERTS), axis=1)
    next_expert = jnp.where(next_expert == N_EXPERTS, experts, next_expert).astype(jnp.int32)

    owner = blk_expert[:, None] == experts[None, :]
    blk_rows = jnp.sum(jnp.where(owner, counts + pstarts, 0), axis=1) - blk_row0
    blk_rows = jnp.where(blk_row0 < pends[-1], jnp.clip(blk_rows, 0, ROW_BLOCK), 0).astype(jnp.int32)

    xs = _sc_dispatch(h, dest0, dest1)
    y = _experts(layer, blk_expert, blk_rows, next_expert, n_used, xs, w_gate, w_up, w_down)
    return _sc_gather(y, dest0, dest1), rg


def kernel(x, norm_mix, norm_ffn, norm_final, ab_w_in, ab_w_out, pool_w, pool_scale, gm_w_in, gm_norm, gm_ws, gm_bs, gm_w_out, router_group_w, router_group_b, router_expert_w, router_expert_b, moe_w_gate, moe_w_up, moe_w_down):
    assert x.shape == (BATCH, SEQ, D_MODEL) and x.dtype == F32
    xf = x.reshape(N_TOK, D_MODEL)
    moe = None
    for layer in range(DEPTH):
        i = layer // 2
        if layer % 2 == 0:
            xf, zp, qkv = _ab_in(xf, moe, norm_mix[layer][None, :], ab_w_in[i].astype(BF16))
            o = _attention(qkv.reshape(BATCH, SEQ, 3 * ATTN_WIDTH))
            xf = _ab_out(zp.reshape(BATCH, SEQ, POOL_WIDTH), o, xf.reshape(BATCH, SEQ, D_MODEL),
                         pool_w[i].astype(BF16), pool_scale[i][None, :],
                         ab_w_out[i].astype(BF16)).reshape(N_TOK, D_MODEL)
        else:
            bs_bcast = jnp.broadcast_to(gm_bs[i][:, :, None], (GM_GROUPS, GM_CHUNK, GM_CHUNK))
            xf = _gmlp(xf, moe, norm_mix[layer][None, :], gm_w_in[i].astype(BF16), gm_norm[i][None, :],
                       gm_ws[i], bs_bcast, gm_w_out[i].astype(BF16))
        moe = _moe(layer, xf, norm_ffn[layer], router_group_w[layer], router_group_b[layer],
                   router_expert_w[layer], router_expert_b[layer], moe_w_gate, moe_w_up, moe_w_down)
    return _combine(moe[0], xf, moe[1], norm_final[None, :]).reshape(BATCH, SEQ, D_MODEL)
```

```python
import functools

import jax
import jax.numpy as jnp
from jax import lax
from jax.experimental import pallas as pl
from jax.experimental.pallas import tpu as pltpu
from jax.experimental.pallas import tpu_sc as plsc

D_MODEL = 1024
BATCH = 8
SEQ = 2048
DEPTH = 4
N_TOK = BATCH * SEQ

POOL_WINDOWS = (2, 4, 8, 16)
POOL_CH = 128
POOL_WIDTH = 512
POOL_HALO = 16
HEAD_DIM = 64
ATTN_WIDTH = 512
HEADS_PER_STEP = 2
ATTN_BLOCK = 128
GM_CHUNK = 128
GM_GROUPS = 8
GM_WIDTH = 1024
N_GROUPS = 8
N_EXPERTS = 64
TOP_K = 2
D_EXPERT = 512
ROW_BLOCK = 128
N_ROWS = (N_TOK * TOP_K + N_EXPERTS * (ROW_BLOCK - 1) + ROW_BLOCK - 1) // ROW_BLOCK * ROW_BLOCK
N_BLK = N_ROWS // ROW_BLOCK
EPS = 1e-6
NEG = -1e30

LANES = 128
SUBLANES = 8
TOKEN_TILE = (SUBLANES, LANES)
assert SUBLANES * LANES == D_MODEL
V7X_VMEM_BYTES = 64 * 1024 * 1024
MIB = 1024 * 1024

F32 = jnp.float32
BF16 = jnp.bfloat16


def _params(semantics, vmem_mib):
    assert vmem_mib * MIB < V7X_VMEM_BYTES
    return pltpu.CompilerParams(dimension_semantics=semantics, vmem_limit_bytes=vmem_mib * MIB)


def _rms(x, g):
    return x * lax.rsqrt(jnp.mean(x * x, axis=-1, keepdims=True) + EPS) * g


def _dot(a, b):
    return jnp.dot(a, b, preferred_element_type=F32)


def _dot_nt(a, b):
    return lax.dot_general(a, b, (((1,), (1,)), ((), ())), preferred_element_type=F32)


AB_TM = 512


def _moe_specs(rows):
    return [
        pl.BlockSpec((None, rows, *TOKEN_TILE), lambda i: (0, i, 0, 0)),
        pl.BlockSpec((None, rows, *TOKEN_TILE), lambda i: (1, i, 0, 0)),
        pl.BlockSpec((rows, ROUTER_LANES), lambda i: (i, 0)),
    ]


def _add_moe(x, y0_ref, y1_ref, rg_ref):
    rows = x.shape[0]
    rg = rg_ref[...]
    return (x + rg[:, 0:1] * y0_ref[...].reshape(rows, D_MODEL)
            + rg[:, 1:2] * y1_ref[...].reshape(rows, D_MODEL))


def _ab_in_body(has_moe, *refs):
    if has_moe:
        y0_ref, y1_ref, rg_ref, x_ref, g_ref, w_ref, xnew_ref, zp_ref, qkv_ref = refs
        x = _add_moe(x_ref[...], y0_ref, y1_ref, rg_ref)
        xnew_ref[...] = x
    else:
        x_ref, g_ref, w_ref, zp_ref, qkv_ref = refs
        x = x_ref[...]
    h = _rms(x, g_ref[...]).astype(BF16)
    z = _dot(h, w_ref[...])
    zp_ref[...] = z[:, :POOL_WIDTH]
    q = z[:, POOL_WIDTH:POOL_WIDTH + ATTN_WIDTH] * (HEAD_DIM ** -0.5)
    qkv_ref[...] = jnp.concatenate([q, z[:, POOL_WIDTH + ATTN_WIDTH:]], axis=1).astype(BF16)


def _ab_in(x, moe, g, w_bf16):
    n = x.shape[0]
    has_moe = moe is not None
    row_spec = pl.BlockSpec((AB_TM, D_MODEL), lambda i: (i, 0))
    outs = pl.pallas_call(
        functools.partial(_ab_in_body, has_moe),
        grid=(n // AB_TM,),
        in_specs=(_moe_specs(AB_TM) if has_moe else []) + [
            row_spec,
            pl.BlockSpec((1, D_MODEL), lambda i: (0, 0)),
            pl.BlockSpec((D_MODEL, 4 * ATTN_WIDTH), lambda i: (0, 0)),
        ],
        out_specs=([row_spec] if has_moe else []) + [
            pl.BlockSpec((AB_TM, POOL_WIDTH), lambda i: (i, 0)),
            pl.BlockSpec((AB_TM, 3 * ATTN_WIDTH), lambda i: (i, 0)),
        ],
        out_shape=([jax.ShapeDtypeStruct((n, D_MODEL), F32)] if has_moe else []) + [
            jax.ShapeDtypeStruct((n, POOL_WIDTH), F32),
            jax.ShapeDtypeStruct((n, 3 * ATTN_WIDTH), BF16),
        ],
        compiler_params=_params(("arbitrary",), 48),
        name="ab_in",
    )(*((moe[0], moe[0], moe[1]) if has_moe else ()), x, g, w_bf16)
    return tuple(outs) if has_moe else (x, *outs)


ATTN_UNROLL = 16


def _attn_body(q_ref, k_ref, v_ref, o_ref, qf, kf, vf, o1, o4, o16, l1, l4, l16):
    qf[...] = q_ref[...].astype(F32)
    kf[...] = k_ref[...].astype(F32)
    vf[...] = v_ref[...].astype(F32)

    key = lax.broadcasted_iota(jnp.int32, (ATTN_BLOCK, HEADS_PER_STEP * ATTN_BLOCK), 0)
    qry = lax.broadcasted_iota(jnp.int32, (ATTN_BLOCK, HEADS_PER_STEP * ATTN_BLOCK), 1) % ATTN_BLOCK
    cur_ok = key <= qry
    prev_ok = key >= qry
    head0_lane = lax.broadcasted_iota(jnp.int32, (ATTN_BLOCK, LANES), 1) < HEAD_DIM

    def rows(ref, start, d):
        if d == 1:
            return ref[pl.ds(start, ATTN_BLOCK), :]
        return ref[pl.ds(start, ATTN_BLOCK, stride=d), :]

    def put(ref, start, d, val):
        if d == 1:
            ref[pl.ds(start, ATTN_BLOCK), :] = val
        else:
            ref[pl.ds(start, ATTN_BLOCK, stride=d), :] = val

    for d, o_acc, l_acc in ((1, o1, l1), (4, o4, l4), (16, o16, l16)):
        nb = SEQ // d // ATTN_BLOCK
        has_prev = nb > 1

        n_blocks = SEQ // ATTN_BLOCK
        halves = [slice(hh * ATTN_BLOCK, (hh + 1) * ATTN_BLOCK) for hh in range(HEADS_PER_STEP)]
        chans = [slice(hh * HEAD_DIM, (hh + 1) * HEAD_DIM) for hh in range(HEADS_PER_STEP)]

        def block_start(idx, d=d, nb=nb):
            return idx // nb + (idx % nb) * (ATTN_BLOCK * d)

        def scores(idx, k_prev, d=d, has_prev=has_prev):
            start = block_start(idx)
            qv = rows(qf, start, d)
            q2 = jnp.concatenate([jnp.where(head0_lane, qv, 0.0), jnp.where(head0_lane, 0.0, qv)],
                                 axis=0).astype(BF16)
            kc = rows(kf, start, d).astype(BF16)
            keys = jnp.concatenate([k_prev, kc], axis=0) if has_prev else kc
            return _dot_nt(keys, q2), kc

        def finish(pv, den, lse, start, d=d, o_acc=o_acc, l_acc=l_acc):
            out_t = jnp.concatenate([pv[c, h] / den[:, h] for c, h in zip(chans, halves)], axis=0)
            lse_t = jnp.concatenate([jnp.broadcast_to(lse[:, h], (HEAD_DIM, ATTN_BLOCK)) for h in halves],
                                    axis=0)
            put(o_acc, start, d, out_t.T)
            put(l_acc, start, d, lse_t.T)

        def block(idx, carry, d=d, nb=nb, has_prev=has_prev):
            s_raw, k_cur, v_prev_t, pv_last, den_last, lse_last, start_last = carry
            s_next, k_next = scores(jnp.minimum(idx + 1, n_blocks - 1), k_cur)
            finish(pv_last, den_last, lse_last, start_last)
            start = block_start(idx)
            vc_t = rows(vf, start, d).T.astype(BF16)
            if has_prev:
                vals_t = jnp.concatenate([v_prev_t, vc_t], axis=1)
                ok = jnp.concatenate([jnp.logical_and(prev_ok, idx % nb > 0), cur_ok], axis=0)
            else:
                vals_t, ok = vc_t, cur_ok
            s = jnp.where(ok, s_raw, NEG)
            m = jnp.max(s, axis=0, keepdims=True)
            p = jnp.exp(s - m)
            den = jnp.sum(p, axis=0, keepdims=True)
            pv = _dot(vals_t, p.astype(BF16))
            return s_next, k_next, vc_t, pv, den, m + jnp.log(den), start

        k_none = jnp.zeros((ATTN_BLOCK, LANES), BF16)
        s_first, k_first = scores(0, k_none)
        width = HEADS_PER_STEP * ATTN_BLOCK
        init = (s_first, k_first, jnp.zeros((LANES, ATTN_BLOCK), BF16), jnp.zeros((LANES, width), F32),
                jnp.ones((1, width), F32), jnp.zeros((1, width), F32), jnp.int32(0))
        last = lax.fori_loop(0, n_blocks, block, init, unroll=ATTN_UNROLL)
        finish(*last[3:])

    def merge(c, carry):
        sl = pl.ds(pl.multiple_of(c * ATTN_BLOCK, ATTN_BLOCK), ATTN_BLOCK)
        la, lb, lc = l1[sl, :], l4[sl, :], l16[sl, :]
        mx = jnp.maximum(jnp.maximum(la, lb), lc)
        wa, wb, wc = jnp.exp(la - mx), jnp.exp(lb - mx), jnp.exp(lc - mx)
        num = wa * o1[sl, :] + wb * o4[sl, :] + wc * o16[sl, :]
        o_ref[sl, :] = (num / (wa + wb + wc)).astype(o_ref.dtype)
        return carry

    lax.fori_loop(0, SEQ // ATTN_BLOCK, merge, 0)


def _attention(qkv):
    n_hp = ATTN_WIDTH // LANES
    blk = (None, SEQ, LANES)
    return pl.pallas_call(
        _attn_body,
        grid=(qkv.shape[0], n_hp),
        in_specs=[
            pl.BlockSpec(blk, lambda b, hp: (b, 0, hp)),
            pl.BlockSpec(blk, lambda b, hp: (b, 0, n_hp + hp)),
            pl.BlockSpec(blk, lambda b, hp: (b, 0, 2 * n_hp + hp)),
        ],
        out_specs=pl.BlockSpec(blk, lambda b, hp: (b, 0, hp)),
        out_shape=jax.ShapeDtypeStruct((qkv.shape[0], SEQ, ATTN_WIDTH), BF16),
        scratch_shapes=[pltpu.VMEM((SEQ, LANES), F32) for _ in range(9)],
        compiler_params=_params(("arbitrary", "arbitrary"), 32),
        name="dilated_attn",
    )(qkv, qkv, qkv)


AO_TS = 512


def _ab_out_body(zp_ref, halo_ref, o_ref, x_ref, pw_ref, ps_ref, w_ref, out_ref, zz):
    i = pl.program_id(1)
    zz[0:POOL_HALO, :] = jnp.where(i > 0, halo_ref[...], 0.0)
    zz[POOL_HALO:, :] = zp_ref[...]
    pos = i * AO_TS + lax.broadcasted_iota(jnp.int32, (AO_TS, 1), 0)
    parts = []
    for g, w in enumerate(POOL_WINDOWS):
        cols = slice(g * POOL_CH, (g + 1) * POOL_CH)
        acc = zz[pl.ds(POOL_HALO, AO_TS), cols]
        for j in range(1, w):
            acc = acc + zz[pl.ds(POOL_HALO - j, AO_TS), cols]
        count = jnp.minimum(pos + 1, w).astype(F32)
        pooled = acc / count - zp_ref[:, cols]
        parts.append(_dot(pooled.astype(BF16), pw_ref[g]))
    a = jnp.concatenate(parts, axis=1) * ps_ref[...]
    y = jnp.concatenate([a.astype(BF16), o_ref[...]], axis=1)
    out_ref[...] = x_ref[...] + _dot(y, w_ref[...])


def _ab_out(zp, o, x, pool_w_bf16, pool_scale, w_out_bf16):
    halo_per_tile = AO_TS // POOL_HALO
    return pl.pallas_call(
        _ab_out_body,
        grid=(BATCH, SEQ // AO_TS),
        in_specs=[
            pl.BlockSpec((None, AO_TS, POOL_WIDTH), lambda b, i: (b, i, 0)),
            pl.BlockSpec((None, POOL_HALO, POOL_WIDTH),
                         lambda b, i: (b, jnp.maximum(i * halo_per_tile - 1, 0), 0)),
            pl.BlockSpec((None, AO_TS, ATTN_WIDTH), lambda b, i: (b, i, 0)),
            pl.BlockSpec((None, AO_TS, D_MODEL), lambda b, i: (b, i, 0)),
            pl.BlockSpec((len(POOL_WINDOWS), POOL_CH, POOL_CH), lambda b, i: (0, 0, 0)),
            pl.BlockSpec((1, POOL_WIDTH), lambda b, i: (0, 0)),
            pl.BlockSpec((POOL_WIDTH + ATTN_WIDTH, D_MODEL), lambda b, i: (0, 0)),
        ],
        out_specs=pl.BlockSpec((None, AO_TS, D_MODEL), lambda b, i: (b, i, 0)),
        out_shape=jax.ShapeDtypeStruct((BATCH, SEQ, D_MODEL), F32),
        scratch_shapes=[pltpu.VMEM((AO_TS + POOL_HALO, POOL_WIDTH), F32)],
        compiler_params=_params(("arbitrary", "arbitrary"), 32),
        name="pool_ab_out",
    )(zp, zp, o, x, pool_w_bf16, pool_scale, w_out_bf16)


GM_TS = 256


def _gelu_tanh(x):
    return 0.5 * x * (1.0 + jnp.tanh(0.7978845608028654 * (x + 0.044715 * (x * x * x))))


def _gmlp_body(y0_ref, y1_ref, rg_ref, x_ref, g_ref, win_ref, gn_ref, ws_ref, bs_ref, wout_ref, out_ref):
    x = _add_moe(x_ref[...], y0_ref, y1_ref, rg_ref)
    h = _rms(x, g_ref[...]).astype(BF16)
    z = _gelu_tanh(_dot(h, win_ref[...]))
    u = z[:, :GM_WIDTH]
    v = z[:, GM_WIDTH:]
    vc = v - jnp.mean(v, axis=-1, keepdims=True)
    vn = (vc * lax.rsqrt(jnp.mean(vc * vc, axis=-1, keepdims=True) + EPS) * gn_ref[...]).astype(BF16)
    ri = lax.broadcasted_iota(jnp.int32, (GM_CHUNK, GM_CHUNK), 0)
    ci = lax.broadcasted_iota(jnp.int32, (GM_CHUNK, GM_CHUNK), 1)
    causal = ci <= ri
    rows = []
    for c in range(GM_TS // GM_CHUNK):
        rsl = slice(c * GM_CHUNK, (c + 1) * GM_CHUNK)
        cols = []
        for g in range(GM_GROUPS):
            csl = slice(g * (GM_WIDTH // GM_GROUPS), (g + 1) * (GM_WIDTH // GM_GROUPS))
            ws_c = jnp.where(causal, ws_ref[g], 0.0).astype(BF16)
            sv = _dot(ws_c, vn[rsl, csl]) + bs_ref[g]
            cols.append(u[rsl, csl] * sv)
        rows.append(jnp.concatenate(cols, axis=1))
    y = jnp.concatenate(rows, axis=0).astype(BF16)
    out_ref[...] = x + _dot(y, wout_ref[...])


def _gmlp(x, moe, g, w_in_bf16, gm_norm, ws, bs_bcast, w_out_bf16):
    n = x.shape[0]
    const2 = lambda i: (0, 0)
    const3 = lambda i: (0, 0, 0)
    return pl.pallas_call(
        _gmlp_body,
        grid=(n // GM_TS,),
        in_specs=_moe_specs(GM_TS) + [
            pl.BlockSpec((GM_TS, D_MODEL), lambda i: (i, 0)),
            pl.BlockSpec((1, D_MODEL), const2),
            pl.BlockSpec((D_MODEL, 2 * GM_WIDTH), const2),
            pl.BlockSpec((1, GM_WIDTH), const2),
            pl.BlockSpec((GM_GROUPS, GM_CHUNK, GM_CHUNK), const3),
            pl.BlockSpec((GM_GROUPS, GM_CHUNK, GM_CHUNK), const3),
            pl.BlockSpec((GM_WIDTH, D_MODEL), const2),
        ],
        out_specs=pl.BlockSpec((GM_TS, D_MODEL), lambda i: (i, 0)),
        out_shape=jax.ShapeDtypeStruct((n, D_MODEL), F32),
        compiler_params=_params(("arbitrary",), 40),
        name="gmlp",
    )(moe[0], moe[0], moe[1], x, g, w_in_bf16, gm_norm, ws, bs_bcast, w_out_bf16)


RT_TN = 512
ROUTER_LANES = LANES
EXPERT_LANE0 = N_GROUPS


def _router_body(x_ref, g_ref, wr_ref, br_ref, h_ref, ri_ref, rg_ref, cnt_ref, tri, running):
    step = pl.program_id(0)

    @pl.when(step == 0)
    def _():
        r = lax.broadcasted_iota(jnp.int32, (RT_TN, RT_TN), 0)
        c = lax.broadcasted_iota(jnp.int32, (RT_TN, RT_TN), 1)
        tri[...] = jnp.where(c < r, 1.0, 0.0).astype(BF16)
        running[...] = jnp.zeros_like(running)

    h = _rms(x_ref[...], g_ref[...])
    h_ref[...] = h.reshape(RT_TN, *TOKEN_TILE)
    h_hi = h.astype(BF16)
    h_lo = (h - h_hi.astype(F32)).astype(BF16)
    logits = (_dot(h_hi, wr_ref[0]) + (_dot(h_hi, wr_ref[1]) + _dot(h_lo, wr_ref[0]))) + br_ref[...]
    lane = lax.broadcasted_iota(jnp.int32, (RT_TN, ROUTER_LANES), 1)

    is_group = lane < N_GROUPS
    lg = jnp.where(is_group, logits, NEG)
    mg = jnp.max(lg, axis=1, keepdims=True)
    p_top = 1.0 / jnp.sum(jnp.where(is_group, jnp.exp(logits - mg), 0.0), axis=1, keepdims=True)
    g_idx = jnp.min(jnp.where(lg == mg, lane, ROUTER_LANES), axis=1, keepdims=True)

    in_group = jnp.logical_and(
        jnp.logical_and(lane >= EXPERT_LANE0, lane < EXPERT_LANE0 + N_EXPERTS),
        ((lane - EXPERT_LANE0) >> 3) == g_idx)
    le = jnp.where(in_group, logits, NEG)
    t1 = jnp.max(le, axis=1, keepdims=True)
    i1 = jnp.min(jnp.where(le == t1, lane, ROUTER_LANES), axis=1, keepdims=True)
    le2 = jnp.where(lane == i1, NEG, le)
    t2 = jnp.max(le2, axis=1, keepdims=True)
    i2 = jnp.min(jnp.where(le2 == t2, lane, ROUTER_LANES), axis=1, keepdims=True)
    e0 = i1 - EXPERT_LANE0
    e1 = i2 - EXPERT_LANE0
    ex = jnp.exp(t2 - t1)
    w0 = 1.0 / (1.0 + ex)
    g0 = p_top * w0
    g1 = p_top * (ex * w0)

    oh0 = lane == e0
    oh1 = lane == e1
    cnt = jnp.where(jnp.logical_or(oh0, oh1), 1.0, 0.0)
    base = running[...] + _dot(tri[...], cnt.astype(BF16))
    rank0 = jnp.sum(jnp.where(oh0, base, 0.0), axis=1, keepdims=True).astype(jnp.int32)
    rank1 = jnp.sum(jnp.where(oh1, base, 0.0), axis=1, keepdims=True).astype(jnp.int32)
    running[...] = running[...] + jnp.sum(cnt, axis=0, keepdims=True)
    cnt_ref[...] = running[...]

    ri_ref[...] = jnp.where(lane == 0, e0, jnp.where(lane == 1, e1,
                            jnp.where(lane == 2, rank0, jnp.where(lane == 3, rank1, 0))))
    rg_ref[...] = jnp.where(lane == 0, g0, jnp.where(lane == 1, g1, 0.0))


def _router(x, g, wr, br):
    n = x.shape[0]
    const2 = lambda i: (0, 0)
    return pl.pallas_call(
        _router_body,
        grid=(n // RT_TN,),
        in_specs=[
            pl.BlockSpec((RT_TN, D_MODEL), lambda i: (i, 0)),
            pl.BlockSpec((1, D_MODEL), const2),
            pl.BlockSpec((2, D_MODEL, ROUTER_LANES), lambda i: (0, 0, 0)),
            pl.BlockSpec((1, ROUTER_LANES), const2),
        ],
        out_specs=[
            pl.BlockSpec((RT_TN, *TOKEN_TILE), lambda i: (i, 0, 0)),
            pl.BlockSpec((RT_TN, ROUTER_LANES), lambda i: (i, 0)),
            pl.BlockSpec((RT_TN, ROUTER_LANES), lambda i: (i, 0)),
            pl.BlockSpec((1, ROUTER_LANES), const2),
        ],
        out_shape=[
            jax.ShapeDtypeStruct((n, *TOKEN_TILE), F32),
            jax.ShapeDtypeStruct((n, ROUTER_LANES), jnp.int32),
            jax.ShapeDtypeStruct((n, ROUTER_LANES), F32),
            jax.ShapeDtypeStruct((1, ROUTER_LANES), F32),
        ],
        scratch_shapes=[pltpu.VMEM((RT_TN, RT_TN), BF16), pltpu.VMEM((1, ROUTER_LANES), F32)],
        compiler_params=_params(("arbitrary",), 32),
        name="router",
    )(x, g, wr, br)


SC_CORES = 2
SC_SUBCORES = 16
SC_WORKERS = SC_CORES * SC_SUBCORES
SC_CHUNK = 64


def _sc_dispatch(h, dest0, dest1):
    n = h.shape[0]
    per_worker = n // SC_WORKERS
    assert per_worker % SC_CHUNK == 0
    mesh = plsc.VectorSubcoreMesh(core_axis_name="c", subcore_axis_name="s")

    @functools.partial(
        pl.kernel, mesh=mesh,
        out_type=jax.ShapeDtypeStruct((N_ROWS, *TOKEN_TILE), F32),
        scratch_types=[pltpu.VMEM((SC_CHUNK,), jnp.int32), pltpu.VMEM((SC_CHUNK,), jnp.int32),
                       pltpu.VMEM((SC_CHUNK, *TOKEN_TILE), F32), pltpu.SemaphoreType.DMA],
        name="moe_dispatch_sc")
    def scatter(h_hbm, d0_hbm, d1_hbm, xs_hbm, idx0, idx1, rows, sem):
        worker = lax.axis_index("s") * SC_CORES + lax.axis_index("c")
        base = worker * per_worker
        for c in range(per_worker // SC_CHUNK):
            t0 = base + c * SC_CHUNK
            pltpu.sync_copy(d0_hbm.at[pl.ds(t0, SC_CHUNK)], idx0)
            pltpu.sync_copy(d1_hbm.at[pl.ds(t0, SC_CHUNK)], idx1)
            pltpu.sync_copy(h_hbm.at[pl.ds(t0, SC_CHUNK)], rows)
            first = pltpu.async_copy(rows, xs_hbm.at[idx0], sem)
            second = pltpu.async_copy(rows, xs_hbm.at[idx1], sem)
            first.wait()
            second.wait()

    return scatter(h, dest0, dest1)


SC_GATHER_CHUNK = 32


def _sc_gather(y, dest0, dest1):
    n = dest0.shape[0]
    per_worker = n // SC_WORKERS
    assert per_worker % SC_GATHER_CHUNK == 0
    mesh = plsc.VectorSubcoreMesh(core_axis_name="c", subcore_axis_name="s")
    buf = pltpu.VMEM((SC_GATHER_CHUNK, *TOKEN_TILE), F32)
    idx = pltpu.VMEM((SC_GATHER_CHUNK,), jnp.int32)

    @functools.partial(
        pl.kernel, mesh=mesh,
        out_type=jax.ShapeDtypeStruct((TOP_K, n, *TOKEN_TILE), F32),
        scratch_types=[idx, idx, buf, buf, pltpu.SemaphoreType.DMA, pltpu.SemaphoreType.DMA],
        name="moe_gather_sc")
    def gather(y_hbm, d0_hbm, d1_hbm, out_hbm, idx0, idx1, rows0, rows1, sem0, sem1):
        worker = lax.axis_index("s") * SC_CORES + lax.axis_index("c")
        base = worker * per_worker
        for c in range(per_worker // SC_GATHER_CHUNK):
            t0 = base + c * SC_GATHER_CHUNK
            pltpu.sync_copy(d0_hbm.at[pl.ds(t0, SC_GATHER_CHUNK)], idx0)
            pltpu.sync_copy(d1_hbm.at[pl.ds(t0, SC_GATHER_CHUNK)], idx1)
            first = pltpu.async_copy(y_hbm.at[idx0], rows0, sem0)
            second = pltpu.async_copy(y_hbm.at[idx1], rows1, sem1)
            first.wait()
            pltpu.sync_copy(rows0, out_hbm.at[0, pl.ds(t0, SC_GATHER_CHUNK)])
            second.wait()
            pltpu.sync_copy(rows1, out_hbm.at[1, pl.ds(t0, SC_GATHER_CHUNK)])

    return gather(y, dest0, dest1)


N_STAGE = 3
WEIGHT_DMA_PRIORITY = 1
X_SLOTS = 4
Y_SLOTS = 4


def _expert_body(layer, blk_expert, blk_rows, next_expert, n_used, xs_hbm, wg_hbm, wu_hbm, wd_hbm,
                 y_hbm, xbuf, ybuf, sg, su, sd, wg, wu, wd, slot_ref, wsem, xsem, ysem):
    i = pl.program_id(0)
    e = blk_expert[i]
    nu = n_used[0]

    def x_copy(blk, slot):
        return pltpu.make_async_copy(xs_hbm.at[pl.ds(blk * ROW_BLOCK, ROW_BLOCK)], xbuf.at[slot],
                                     xsem.at[slot])

    def y_copy(blk, slot):
        return pltpu.make_async_copy(ybuf.at[slot], y_hbm.at[pl.ds(blk * ROW_BLOCK, ROW_BLOCK)],
                                     ysem.at[slot])

    def fetch(expert, slot):
        return (pltpu.make_async_copy(wg_hbm.at[layer, expert], sg.at[slot], wsem.at[slot, 0]),
                pltpu.make_async_copy(wu_hbm.at[layer, expert], su.at[slot], wsem.at[slot, 1]),
                pltpu.make_async_copy(wd_hbm.at[layer, expert], sd.at[slot], wsem.at[slot, 2]))

    def start_fetch(expert, slot):
        for c in fetch(expert, slot):
            c.start(priority=WEIGHT_DMA_PRIORITY)

    nxt = next_expert[e]
    nxt2 = next_expert[nxt]

    @pl.when(jnp.logical_and(i == 0, nu > 0))
    def _():
        slot_ref[0] = 0
        start_fetch(e, 0)

        @pl.when(nxt != e)
        def _():
            start_fetch(nxt, 1)

        for b in range(X_SLOTS - 1):
            @pl.when(b < nu)
            def _():
                x_copy(b, b).start()

    used = i < nu
    first_of_expert = jnp.logical_or(i == 0, blk_expert[jnp.maximum(i - 1, 0)] != e)
    xslot = i % X_SLOTS
    yslot = i % Y_SLOTS

    @pl.when(i >= Y_SLOTS)
    def _():
        y_copy(i - Y_SLOTS, yslot).wait()

    @pl.when(jnp.logical_and(used, first_of_expert))
    def _():
        slot = slot_ref[0]

        @pl.when(nxt2 != nxt)
        def _():
            start_fetch(nxt2, (slot + N_STAGE - 1) % N_STAGE)

        for c in fetch(e, slot):
            c.wait()
        wg[...] = sg[slot].astype(BF16)
        wu[...] = su[slot].astype(BF16)
        wd[...] = sd[slot].astype(BF16)
        slot_ref[0] = (slot + 1) % N_STAGE

    @pl.when(used)
    def _():
        ahead = i + X_SLOTS - 1

        @pl.when(ahead < nu)
        def _():
            x_copy(ahead, ahead % X_SLOTS).start()

        x_copy(i, xslot).wait()
        row = lax.broadcasted_iota(jnp.int32, (ROW_BLOCK, 1), 0)
        x = jnp.where(row < blk_rows[i], xbuf[xslot].reshape(ROW_BLOCK, D_MODEL), 0.0).astype(BF16)
        a = _dot(x, wg[...])
        b = _dot(x, wu[...])
        hmid = (a * (1.0 / (1.0 + jnp.exp(-a)))) * b
        ybuf[yslot] = _dot(hmid.astype(BF16), wd[...]).reshape(ROW_BLOCK, *TOKEN_TILE)

    @pl.when(jnp.logical_not(used))
    def _():
        ybuf[yslot] = jnp.zeros((ROW_BLOCK, *TOKEN_TILE), F32)

    y_copy(i, yslot).start()

    @pl.when(i == N_BLK - 1)
    def _():
        for s in range(Y_SLOTS):
            y_copy(0, s).wait()


def _experts(layer, blk_expert, blk_rows, next_expert, n_used, xs, w_gate, w_up, w_down):
    assert N_BLK >= Y_SLOTS
    hbm = pl.BlockSpec(memory_space=pl.ANY)
    return pl.pallas_call(
        functools.partial(_expert_body, layer),
        grid_spec=pltpu.PrefetchScalarGridSpec(
            num_scalar_prefetch=4,
            grid=(N_BLK,),
            in_specs=[hbm, hbm, hbm, hbm],
            out_specs=hbm,
            scratch_shapes=[
                pltpu.VMEM((X_SLOTS, ROW_BLOCK, *TOKEN_TILE), F32),
                pltpu.VMEM((Y_SLOTS, ROW_BLOCK, *TOKEN_TILE), F32),
                pltpu.VMEM((N_STAGE, D_MODEL, D_EXPERT), F32),
                pltpu.VMEM((N_STAGE, D_MODEL, D_EXPERT), F32),
                pltpu.VMEM((N_STAGE, D_EXPERT, D_MODEL), F32),
                pltpu.VMEM((D_MODEL, D_EXPERT), BF16),
                pltpu.VMEM((D_MODEL, D_EXPERT), BF16),
                pltpu.VMEM((D_EXPERT, D_MODEL), BF16),
                pltpu.SMEM((1,), jnp.int32),
                pltpu.SemaphoreType.DMA((N_STAGE, 3)),
                pltpu.SemaphoreType.DMA((X_SLOTS,)),
                pltpu.SemaphoreType.DMA((Y_SLOTS,)),
            ],
        ),
        out_shape=jax.ShapeDtypeStruct((N_ROWS, *TOKEN_TILE), F32),
        compiler_params=_params(("arbitrary",), 40),
        name="moe_experts",
    )(blk_expert, blk_rows, next_expert, n_used, xs, w_gate, w_up, w_down)


CB_TC = 512


def _combine_body(y0_ref, y1_ref, rg_ref, x_ref, fg_ref, out_ref):
    out_ref[...] = _rms(_add_moe(x_ref[...], y0_ref, y1_ref, rg_ref), fg_ref[...])


def _combine(y2, x, rg, final_g):
    n = x.shape[0]
    return pl.pallas_call(
        _combine_body,
        grid=(n // CB_TC,),
        in_specs=_moe_specs(CB_TC) + [
            pl.BlockSpec((CB_TC, D_MODEL), lambda i: (i, 0)),
            pl.BlockSpec((1, D_MODEL), lambda i: (0, 0)),
        ],
        out_specs=pl.BlockSpec((CB_TC, D_MODEL), lambda i: (i, 0)),
        out_shape=jax.ShapeDtypeStruct((n, D_MODEL), F32),
        compiler_params=_params(("arbitrary",), 32),
        name="moe_combine",
    )(y2, y2, rg, x, final_g)


def _moe(layer, x, norm_g, group_w, group_b, exp_w, exp_b, w_gate, w_up, w_down):
    pad_w = ROUTER_LANES - N_GROUPS - N_EXPERTS
    wr = jnp.concatenate([group_w, exp_w, jnp.zeros((D_MODEL, pad_w), F32)], axis=1)
    br = jnp.concatenate([group_b, exp_b, jnp.zeros((pad_w,), F32)])[None, :]
    wr_hi = wr.astype(BF16)
    wr_lo = (wr - wr_hi.astype(F32)).astype(BF16)
    h, ri, rg, cnt = _router(x, norm_g[None, :], jnp.stack([wr_hi, wr_lo]), br)

    experts = jnp.arange(N_EXPERTS, dtype=jnp.int32)
    counts = cnt[0, :N_EXPERTS].astype(jnp.int32)
    pcounts = (counts + ROW_BLOCK - 1) // ROW_BLOCK * ROW_BLOCK
    pends = jnp.cumsum(pcounts)
    pstarts = pends - pcounts
    dest0, dest1 = (
        (jnp.sum(jnp.where(ri[:, s, None] == experts, pstarts, 0), axis=-1) + ri[:, TOP_K + s]).astype(jnp.int32)
        for s in range(TOP_K))
    blk_row0 = jnp.arange(N_BLK, dtype=jnp.int32) * ROW_BLOCK
    blk_expert = jnp.minimum(jnp.sum(pends[None, :] <= blk_row0[:, None], axis=1),
                             N_EXPERTS - 1).astype(jnp.int32)
    n_used = (pends[-1:] // ROW_BLOCK).astype(jnp.int32)
    later_used = jnp.logical_and(experts[None, :] > experts[:, None], (pcounts > 0)[None, :])
    next_expert = jnp.min(jnp.where(later_used, experts[None, :], N_EXPERTS), axis=1)
    next_expert = jnp.where(next_expert == N_EXPERTS, experts, next_expert).astype(jnp.int32)

    owner = blk_expert[:, None] == experts[None, :]
    blk_rows = jnp.sum(jnp.where(owner, counts + pstarts, 0), axis=1) - blk_row0
    blk_rows = jnp.where(blk_row0 < pends[-1], jnp.clip(blk_rows, 0, ROW_BLOCK), 0).astype(jnp.int32)

    xs = _sc_dispatch(h, dest0, dest1)
    y = _experts(layer, blk_expert, blk_rows, next_expert, n_used, xs, w_gate, w_up, w_down)
    return _sc_gather(y, dest0, dest1), rg


def kernel(x, norm_mix, norm_ffn, norm_final, ab_w_in, ab_w_out, pool_w, pool_scale, gm_w_in, gm_norm, gm_ws, gm_bs, gm_w_out, router_group_w, router_group_b, router_expert_w, router_expert_b, moe_w_gate, moe_w_up, moe_w_down):
    assert x.shape == (BATCH, SEQ, D_MODEL) and x.dtype == F32
    xf = x.reshape(N_TOK, D_MODEL)
    moe = None
    for layer in range(DEPTH):
        i = layer // 2
        if layer % 2 == 0:
            xf, zp, qkv = _ab_in(xf, moe, norm_mix[layer][None, :], ab_w_in[i].astype(BF16))
            o = _attention(qkv.reshape(BATCH, SEQ, 3 * ATTN_WIDTH))
            xf = _ab_out(zp.reshape(BATCH, SEQ, POOL_WIDTH), o, xf.reshape(BATCH, SEQ, D_MODEL),
                         pool_w[i].astype(BF16), pool_scale[i][None, :],
                         ab_w_out[i].astype(BF16)).reshape(N_TOK, D_MODEL)
        else:
            bs_bcast = jnp.broadcast_to(gm_bs[i][:, :, None], (GM_GROUPS, GM_CHUNK, GM_CHUNK))
            xf = _gmlp(xf, moe, norm_mix[layer][None, :], gm_w_in[i].astype(BF16), gm_norm[i][None, :],
                       gm_ws[i], bs_bcast, gm_w_out[i].astype(BF16))
        moe = _moe(layer, xf, norm_ffn[layer], router_group_w[layer], router_group_b[layer],
                   router_expert_w[layer], router_expert_b[layer], moe_w_gate, moe_w_up, moe_w_down)
    return _combine(moe[0], xf, moe[1], norm_final[None, :]).reshape(BATCH, SEQ, D_MODEL)
```

```python
import functools

import jax
import jax.numpy as jnp
from jax import lax
from jax.experimental import pallas as pl
from jax.experimental.pallas import tpu as pltpu
from jax.experimental.pallas import tpu_sc as plsc

D_MODEL = 1024
BATCH = 8
SEQ = 2048
DEPTH = 4
N_TOK = BATCH * SEQ

POOL_WINDOWS = (2, 4, 8, 16)
POOL_CH = 128
POOL_WIDTH = 512
POOL_HALO = 16
HEAD_DIM = 64
ATTN_WIDTH = 512
HEADS_PER_STEP = 2
ATTN_BLOCK = 128
GM_CHUNK = 128
GM_GROUPS = 8
GM_WIDTH = 1024
N_GROUPS = 8
N_EXPERTS = 64
TOP_K = 2
D_EXPERT = 512
ROW_BLOCK = 128
N_ROWS = (N_TOK * TOP_K + N_EXPERTS * (ROW_BLOCK - 1) + ROW_BLOCK - 1) // ROW_BLOCK * ROW_BLOCK
N_BLK = N_ROWS // ROW_BLOCK
EPS = 1e-6
NEG = -1e30

LANES = 128
SUBLANES = 8
TOKEN_TILE = (SUBLANES, LANES)
assert SUBLANES * LANES == D_MODEL
V7X_VMEM_BYTES = 64 * 1024 * 1024
MIB = 1024 * 1024

F32 = jnp.float32
BF16 = jnp.bfloat16


def _params(semantics, vmem_mib):
    assert vmem_mib * MIB < V7X_VMEM_BYTES
    return pltpu.CompilerParams(dimension_semantics=semantics, vmem_limit_bytes=vmem_mib * MIB)


def _rms(x, g):
    return x * lax.rsqrt(jnp.mean(x * x, axis=-1, keepdims=True) + EPS) * g


def _dot(a, b):
    return jnp.dot(a, b, preferred_element_type=F32)


def _dot_nt(a, b):
    return lax.dot_general(a, b, (((1,), (1,)), ((), ())), preferred_element_type=F32)


AB_TM = 512


def _moe_specs(rows):
    return [
        pl.BlockSpec((None, rows, *TOKEN_TILE), lambda i: (0, i, 0, 0)),
        pl.BlockSpec((None, rows, *TOKEN_TILE), lambda i: (1, i, 0, 0)),
        pl.BlockSpec((rows, ROUTER_LANES), lambda i: (i, 0)),
    ]


def _add_moe(x, y0_ref, y1_ref, rg_ref):
    rows = x.shape[0]
    rg = rg_ref[...]
    return (x + rg[:, 0:1] * y0_ref[...].reshape(rows, D_MODEL)
            + rg[:, 1:2] * y1_ref[...].reshape(rows, D_MODEL))


def _ab_in_body(has_moe, *refs):
    if has_moe:
        y0_ref, y1_ref, rg_ref, x_ref, g_ref, w_ref, xnew_ref, zp_ref, qkv_ref = refs
        x = _add_moe(x_ref[...], y0_ref, y1_ref, rg_ref)
        xnew_ref[...] = x
    else:
        x_ref, g_ref, w_ref, zp_ref, qkv_ref = refs
        x = x_ref[...]
    h = _rms(x, g_ref[...]).astype(BF16)
    z = _dot(h, w_ref[...])
    zp_ref[...] = z[:, :POOL_WIDTH]
    q = z[:, POOL_WIDTH:POOL_WIDTH + ATTN_WIDTH] * (HEAD_DIM ** -0.5)
    qkv_ref[...] = jnp.concatenate([q, z[:, POOL_WIDTH + ATTN_WIDTH:]], axis=1).astype(BF16)


def _ab_in(x, moe, g, w_bf16):
    n = x.shape[0]
    has_moe = moe is not None
    row_spec = pl.BlockSpec((AB_TM, D_MODEL), lambda i: (i, 0))
    outs = pl.pallas_call(
        functools.partial(_ab_in_body, has_moe),
        grid=(n // AB_TM,),
        in_specs=(_moe_specs(AB_TM) if has_moe else []) + [
            row_spec,
            pl.BlockSpec((1, D_MODEL), lambda i: (0, 0)),
            pl.BlockSpec((D_MODEL, 4 * ATTN_WIDTH), lambda i: (0, 0)),
        ],
        out_specs=([row_spec] if has_moe else []) + [
            pl.BlockSpec((AB_TM, POOL_WIDTH), lambda i: (i, 0)),
            pl.BlockSpec((AB_TM, 3 * ATTN_WIDTH), lambda i: (i, 0)),
        ],
        out_shape=([jax.ShapeDtypeStruct((n, D_MODEL), F32)] if has_moe else []) + [
            jax.ShapeDtypeStruct((n, POOL_WIDTH), F32),
            jax.ShapeDtypeStruct((n, 3 * ATTN_WIDTH), BF16),
        ],
        compiler_params=_params(("arbitrary",), 48),
        name="ab_in",
    )(*((moe[0], moe[0], moe[1]) if has_moe else ()), x, g, w_bf16)
    return tuple(outs) if has_moe else (x, *outs)


ATTN_UNROLL = 16
MERGE_UNROLL = 4


def _attn_body(q_ref, k_ref, v_ref, o_ref, qf, kf, vf, o1, o4, o16, l1, l4, l16):
    qf[...] = q_ref[...].astype(F32)
    kf[...] = k_ref[...].astype(F32)
    vf[...] = v_ref[...].astype(F32)

    key = lax.broadcasted_iota(jnp.int32, (ATTN_BLOCK, HEADS_PER_STEP * ATTN_BLOCK), 0)
    qry = lax.broadcasted_iota(jnp.int32, (ATTN_BLOCK, HEADS_PER_STEP * ATTN_BLOCK), 1) % ATTN_BLOCK
    cur_ok = key <= qry
    prev_ok = key >= qry
    head0_lane = lax.broadcasted_iota(jnp.int32, (ATTN_BLOCK, LANES), 1) < HEAD_DIM

    def rows(ref, start, d):
        if d == 1:
            return ref[pl.ds(start, ATTN_BLOCK), :]
        return ref[pl.ds(start, ATTN_BLOCK, stride=d), :]

    def put(ref, start, d, val):
        if d == 1:
            ref[pl.ds(start, ATTN_BLOCK), :] = val
        else:
            ref[pl.ds(start, ATTN_BLOCK, stride=d), :] = val

    for d, o_acc, l_acc in ((1, o1, l1), (4, o4, l4), (16, o16, l16)):
        nb = SEQ // d // ATTN_BLOCK
        has_prev = nb > 1

        n_blocks = SEQ // ATTN_BLOCK
        halves = [slice(hh * ATTN_BLOCK, (hh + 1) * ATTN_BLOCK) for hh in range(HEADS_PER_STEP)]
        chans = [slice(hh * HEAD_DIM, (hh + 1) * HEAD_DIM) for hh in range(HEADS_PER_STEP)]

        def block_start(idx, d=d, nb=nb):
            return idx // nb + (idx % nb) * (ATTN_BLOCK * d)

        def scores(idx, k_prev, d=d, has_prev=has_prev):
            start = block_start(idx)
            qv = rows(qf, start, d)
            q2 = jnp.concatenate([jnp.where(head0_lane, qv, 0.0), jnp.where(head0_lane, 0.0, qv)],
                                 axis=0).astype(BF16)
            kc = rows(kf, start, d).astype(BF16)
            keys = jnp.concatenate([k_prev, kc], axis=0) if has_prev else kc
            return _dot_nt(keys, q2), kc

        def finish(pv, den, lse, start, d=d, o_acc=o_acc, l_acc=l_acc):
            out_t = jnp.concatenate([pv[c, h] / den[:, h] for c, h in zip(chans, halves)], axis=0)
            lse_t = jnp.concatenate([jnp.broadcast_to(lse[:, h], (HEAD_DIM, ATTN_BLOCK)) for h in halves],
                                    axis=0)
            put(o_acc, start, d, out_t.T)
            put(l_acc, start, d, lse_t.T)

        def block(idx, carry, d=d, nb=nb, has_prev=has_prev):
            s_raw, k_cur, v_prev_t, pv_last, den_last, lse_last, start_last = carry
            s_next, k_next = scores(jnp.minimum(idx + 1, n_blocks - 1), k_cur)
            finish(pv_last, den_last, lse_last, start_last)
            start = block_start(idx)
            vc_t = rows(vf, start, d).T.astype(BF16)
            if has_prev:
                vals_t = jnp.concatenate([v_prev_t, vc_t], axis=1)
                ok = jnp.concatenate([jnp.logical_and(prev_ok, idx % nb > 0), cur_ok], axis=0)
            else:
                vals_t, ok = vc_t, cur_ok
            s = jnp.where(ok, s_raw, NEG)
            m = jnp.max(s, axis=0, keepdims=True)
            p = jnp.exp(s - m)
            den = jnp.sum(p, axis=0, keepdims=True)
            pv = _dot(vals_t, p.astype(BF16))
            return s_next, k_next, vc_t, pv, den, m + jnp.log(den), start

        k_none = jnp.zeros((ATTN_BLOCK, LANES), BF16)
        s_first, k_first = scores(0, k_none)
        width = HEADS_PER_STEP * ATTN_BLOCK
        init = (s_first, k_first, jnp.zeros((LANES, ATTN_BLOCK), BF16), jnp.zeros((LANES, width), F32),
                jnp.ones((1, width), F32), jnp.zeros((1, width), F32), jnp.int32(0))
        last = lax.fori_loop(0, n_blocks, block, init, unroll=ATTN_UNROLL)
        finish(*last[3:])

    def merge(c, carry):
        sl = pl.ds(pl.multiple_of(c * ATTN_BLOCK, ATTN_BLOCK), ATTN_BLOCK)
        la, lb, lc = l1[sl, :], l4[sl, :], l16[sl, :]
        mx = jnp.maximum(jnp.maximum(la, lb), lc)
        wa, wb, wc = jnp.exp(la - mx), jnp.exp(lb - mx), jnp.exp(lc - mx)
        num = wa * o1[sl, :] + wb * o4[sl, :] + wc * o16[sl, :]
        o_ref[sl, :] = (num / (wa + wb + wc)).astype(o_ref.dtype)
        return carry

    lax.fori_loop(0, SEQ // ATTN_BLOCK, merge, 0, unroll=MERGE_UNROLL)


def _attention(qkv):
    n_hp = ATTN_WIDTH // LANES
    blk = (None, SEQ, LANES)
    return pl.pallas_call(
        _attn_body,
        grid=(qkv.shape[0], n_hp),
        in_specs=[
            pl.BlockSpec(blk, lambda b, hp: (b, 0, hp)),
            pl.BlockSpec(blk, lambda b, hp: (b, 0, n_hp + hp)),
            pl.BlockSpec(blk, lambda b, hp: (b, 0, 2 * n_hp + hp)),
        ],
        out_specs=pl.BlockSpec(blk, lambda b, hp: (b, 0, hp)),
        out_shape=jax.ShapeDtypeStruct((qkv.shape[0], SEQ, ATTN_WIDTH), BF16),
        scratch_shapes=[pltpu.VMEM((SEQ, LANES), F32) for _ in range(9)],
        compiler_params=_params(("arbitrary", "arbitrary"), 32),
        name="dilated_attn",
    )(qkv, qkv, qkv)


AO_TS = 512


def _ab_out_body(zp_ref, halo_ref, o_ref, x_ref, pw_ref, ps_ref, w_ref, out_ref, zz):
    i = pl.program_id(1)
    zz[0:POOL_HALO, :] = jnp.where(i > 0, halo_ref[...], 0.0)
    zz[POOL_HALO:, :] = zp_ref[...]
    pos = i * AO_TS + lax.broadcasted_iota(jnp.int32, (AO_TS, 1), 0)
    parts = []
    for g, w in enumerate(POOL_WINDOWS):
        cols = slice(g * POOL_CH, (g + 1) * POOL_CH)
        acc = zz[pl.ds(POOL_HALO, AO_TS), cols]
        for j in range(1, w):
            acc = acc + zz[pl.ds(POOL_HALO - j, AO_TS), cols]
        count = jnp.minimum(pos + 1, w).astype(F32)
        pooled = acc / count - zp_ref[:, cols]
        parts.append(_dot(pooled.astype(BF16), pw_ref[g]))
    a = jnp.concatenate(parts, axis=1) * ps_ref[...]
    y = jnp.concatenate([a.astype(BF16), o_ref[...]], axis=1)
    out_ref[...] = x_ref[...] + _dot(y, w_ref[...])


def _ab_out(zp, o, x, pool_w_bf16, pool_scale, w_out_bf16):
    halo_per_tile = AO_TS // POOL_HALO
    return pl.pallas_call(
        _ab_out_body,
        grid=(BATCH, SEQ // AO_TS),
        in_specs=[
            pl.BlockSpec((None, AO_TS, POOL_WIDTH), lambda b, i: (b, i, 0)),
            pl.BlockSpec((None, POOL_HALO, POOL_WIDTH),
                         lambda b, i: (b, jnp.maximum(i * halo_per_tile - 1, 0), 0)),
            pl.BlockSpec((None, AO_TS, ATTN_WIDTH), lambda b, i: (b, i, 0)),
            pl.BlockSpec((None, AO_TS, D_MODEL), lambda b, i: (b, i, 0)),
            pl.BlockSpec((len(POOL_WINDOWS), POOL_CH, POOL_CH), lambda b, i: (0, 0, 0)),
            pl.BlockSpec((1, POOL_WIDTH), lambda b, i: (0, 0)),
            pl.BlockSpec((POOL_WIDTH + ATTN_WIDTH, D_MODEL), lambda b, i: (0, 0)),
        ],
        out_specs=pl.BlockSpec((None, AO_TS, D_MODEL), lambda b, i: (b, i, 0)),
        out_shape=jax.ShapeDtypeStruct((BATCH, SEQ, D_MODEL), F32),
        scratch_shapes=[pltpu.VMEM((AO_TS + POOL_HALO, POOL_WIDTH), F32)],
        compiler_params=_params(("arbitrary", "arbitrary"), 32),
        name="pool_ab_out",
    )(zp, zp, o, x, pool_w_bf16, pool_scale, w_out_bf16)


GM_TS = 256


GELU_C = 0.7978845608028654
GELU_A = 0.044715


def _gelu_tanh(x):
    return x * (0.5 + 0.5 * jnp.tanh(x * (GELU_C + (GELU_C * GELU_A) * (x * x))))


def _gmlp_body(y0_ref, y1_ref, rg_ref, x_ref, g_ref, win_ref, gn_ref, ws_ref, bs_ref, wout_ref, out_ref):
    x = _add_moe(x_ref[...], y0_ref, y1_ref, rg_ref)
    h = _rms(x, g_ref[...]).astype(BF16)
    z = _gelu_tanh(_dot(h, win_ref[...]))
    u = z[:, :GM_WIDTH]
    v = z[:, GM_WIDTH:]
    vc = v - jnp.mean(v, axis=-1, keepdims=True)
    vn = (vc * lax.rsqrt(jnp.mean(vc * vc, axis=-1, keepdims=True) + EPS) * gn_ref[...]).astype(BF16)
    ri = lax.broadcasted_iota(jnp.int32, (GM_CHUNK, GM_CHUNK), 0)
    ci = lax.broadcasted_iota(jnp.int32, (GM_CHUNK, GM_CHUNK), 1)
    causal = ci <= ri
    rows = []
    for c in range(GM_TS // GM_CHUNK):
        rsl = slice(c * GM_CHUNK, (c + 1) * GM_CHUNK)
        cols = []
        for g in range(GM_GROUPS):
            csl = slice(g * (GM_WIDTH // GM_GROUPS), (g + 1) * (GM_WIDTH // GM_GROUPS))
            ws_c = jnp.where(causal, ws_ref[g], 0.0).astype(BF16)
            sv = _dot(ws_c, vn[rsl, csl]) + bs_ref[g]
            cols.append(u[rsl, csl] * sv)
        rows.append(jnp.concatenate(cols, axis=1))
    y = jnp.concatenate(rows, axis=0).astype(BF16)
    out_ref[...] = x + _dot(y, wout_ref[...])


def _gmlp(x, moe, g, w_in_bf16, gm_norm, ws, bs_bcast, w_out_bf16):
    n = x.shape[0]
    const2 = lambda i: (0, 0)
    const3 = lambda i: (0, 0, 0)
    return pl.pallas_call(
        _gmlp_body,
        grid=(n // GM_TS,),
        in_specs=_moe_specs(GM_TS) + [
            pl.BlockSpec((GM_TS, D_MODEL), lambda i: (i, 0)),
            pl.BlockSpec((1, D_MODEL), const2),
            pl.BlockSpec((D_MODEL, 2 * GM_WIDTH), const2),
            pl.BlockSpec((1, GM_WIDTH), const2),
            pl.BlockSpec((GM_GROUPS, GM_CHUNK, GM_CHUNK), const3),
            pl.BlockSpec((GM_GROUPS, GM_CHUNK, GM_CHUNK), const3),
            pl.BlockSpec((GM_WIDTH, D_MODEL), const2),
        ],
        out_specs=pl.BlockSpec((GM_TS, D_MODEL), lambda i: (i, 0)),
        out_shape=jax.ShapeDtypeStruct((n, D_MODEL), F32),
        compiler_params=_params(("arbitrary",), 40),
        name="gmlp",
    )(moe[0], moe[0], moe[1], x, g, w_in_bf16, gm_norm, ws, bs_bcast, w_out_bf16)


RT_TN = 512
ROUTER_LANES = LANES
EXPERT_LANE0 = N_GROUPS


def _router_body(x_ref, g_ref, wr_ref, br_ref, h_ref, ri_ref, rg_ref, cnt_ref, tri, running):
    step = pl.program_id(0)

    @pl.when(step == 0)
    def _():
        r = lax.broadcasted_iota(jnp.int32, (RT_TN, RT_TN), 0)
        c = lax.broadcasted_iota(jnp.int32, (RT_TN, RT_TN), 1)
        tri[...] = jnp.where(c < r, 1.0, 0.0).astype(BF16)
        running[...] = jnp.zeros_like(running)

    h = _rms(x_ref[...], g_ref[...])
    h_ref[...] = h.reshape(RT_TN, *TOKEN_TILE)
    h_hi = h.astype(BF16)
    h_lo = (h - h_hi.astype(F32)).astype(BF16)
    logits = (_dot(h_hi, wr_ref[0]) + (_dot(h_hi, wr_ref[1]) + _dot(h_lo, wr_ref[0]))) + br_ref[...]
    lane = lax.broadcasted_iota(jnp.int32, (RT_TN, ROUTER_LANES), 1)

    is_group = lane < N_GROUPS
    lg = jnp.where(is_group, logits, NEG)
    mg = jnp.max(lg, axis=1, keepdims=True)
    p_top = 1.0 / jnp.sum(jnp.where(is_group, jnp.exp(logits - mg), 0.0), axis=1, keepdims=True)
    g_idx = jnp.min(jnp.where(lg == mg, lane, ROUTER_LANES), axis=1, keepdims=True)

    in_group = jnp.logical_and(
        jnp.logical_and(lane >= EXPERT_LANE0, lane < EXPERT_LANE0 + N_EXPERTS),
        ((lane - EXPERT_LANE0) >> 3) == g_idx)
    le = jnp.where(in_group, logits, NEG)
    t1 = jnp.max(le, axis=1, keepdims=True)
    i1 = jnp.min(jnp.where(le == t1, lane, ROUTER_LANES), axis=1, keepdims=True)
    le2 = jnp.where(lane == i1, NEG, le)
    t2 = jnp.max(le2, axis=1, keepdims=True)
    i2 = jnp.min(jnp.where(le2 == t2, lane, ROUTER_LANES), axis=1, keepdims=True)
    e0 = i1 - EXPERT_LANE0
    e1 = i2 - EXPERT_LANE0
    ex = jnp.exp(t2 - t1)
    w0 = 1.0 / (1.0 + ex)
    g0 = p_top * w0
    g1 = p_top * (ex * w0)

    oh0 = lane == e0
    oh1 = lane == e1
    cnt = jnp.where(jnp.logical_or(oh0, oh1), 1.0, 0.0)
    base = running[...] + _dot(tri[...], cnt.astype(BF16))
    rank0 = jnp.sum(jnp.where(oh0, base, 0.0), axis=1, keepdims=True).astype(jnp.int32)
    rank1 = jnp.sum(jnp.where(oh1, base, 0.0), axis=1, keepdims=True).astype(jnp.int32)
    running[...] = running[...] + jnp.sum(cnt, axis=0, keepdims=True)
    cnt_ref[...] = running[...]

    ri_ref[...] = jnp.where(lane == 0, e0, jnp.where(lane == 1, e1,
                            jnp.where(lane == 2, rank0, jnp.where(lane == 3, rank1, 0))))
    rg_ref[...] = jnp.where(lane == 0, g0, jnp.where(lane == 1, g1, 0.0))


def _router(x, g, wr, br):
    n = x.shape[0]
    const2 = lambda i: (0, 0)
    return pl.pallas_call(
        _router_body,
        grid=(n // RT_TN,),
        in_specs=[
            pl.BlockSpec((RT_TN, D_MODEL), lambda i: (i, 0)),
            pl.BlockSpec((1, D_MODEL), const2),
            pl.BlockSpec((2, D_MODEL, ROUTER_LANES), lambda i: (0, 0, 0)),
            pl.BlockSpec((1, ROUTER_LANES), const2),
        ],
        out_specs=[
            pl.BlockSpec((RT_TN, *TOKEN_TILE), lambda i: (i, 0, 0)),
            pl.BlockSpec((RT_TN, ROUTER_LANES), lambda i: (i, 0)),
            pl.BlockSpec((RT_TN, ROUTER_LANES), lambda i: (i, 0)),
            pl.BlockSpec((1, ROUTER_LANES), const2),
        ],
        out_shape=[
            jax.ShapeDtypeStruct((n, *TOKEN_TILE), F32),
            jax.ShapeDtypeStruct((n, ROUTER_LANES), jnp.int32),
            jax.ShapeDtypeStruct((n, ROUTER_LANES), F32),
            jax.ShapeDtypeStruct((1, ROUTER_LANES), F32),
        ],
        scratch_shapes=[pltpu.VMEM((RT_TN, RT_TN), BF16), pltpu.VMEM((1, ROUTER_LANES), F32)],
        compiler_params=_params(("arbitrary",), 32),
        name="router",
    )(x, g, wr, br)


SC_CORES = 2
SC_SUBCORES = 16
SC_WORKERS = SC_CORES * SC_SUBCORES
SC_CHUNK = 64


def _sc_dispatch(h, dest0, dest1):
    n = h.shape[0]
    per_worker = n // SC_WORKERS
    assert per_worker % SC_CHUNK == 0
    mesh = plsc.VectorSubcoreMesh(core_axis_name="c", subcore_axis_name="s")

    @functools.partial(
        pl.kernel, mesh=mesh,
        out_type=jax.ShapeDtypeStruct((N_ROWS, *TOKEN_TILE), F32),
        scratch_types=[pltpu.VMEM((SC_CHUNK,), jnp.int32), pltpu.VMEM((SC_CHUNK,), jnp.int32),
                       pltpu.VMEM((SC_CHUNK, *TOKEN_TILE), F32), pltpu.SemaphoreType.DMA],
        name="moe_dispatch_sc")
    def scatter(h_hbm, d0_hbm, d1_hbm, xs_hbm, idx0, idx1, rows, sem):
        worker = lax.axis_index("s") * SC_CORES + lax.axis_index("c")
        base = worker * per_worker
        for c in range(per_worker // SC_CHUNK):
            t0 = base + c * SC_CHUNK
            pltpu.sync_copy(d0_hbm.at[pl.ds(t0, SC_CHUNK)], idx0)
            pltpu.sync_copy(d1_hbm.at[pl.ds(t0, SC_CHUNK)], idx1)
            pltpu.sync_copy(h_hbm.at[pl.ds(t0, SC_CHUNK)], rows)
            first = pltpu.async_copy(rows, xs_hbm.at[idx0], sem)
            second = pltpu.async_copy(rows, xs_hbm.at[idx1], sem)
            first.wait()
            second.wait()

    return scatter(h, dest0, dest1)


SC_GATHER_CHUNK = 32


def _sc_gather(y, dest0, dest1):
    n = dest0.shape[0]
    per_worker = n // SC_WORKERS
    assert per_worker % SC_GATHER_CHUNK == 0
    mesh = plsc.VectorSubcoreMesh(core_axis_name="c", subcore_axis_name="s")
    buf = pltpu.VMEM((SC_GATHER_CHUNK, *TOKEN_TILE), F32)
    idx = pltpu.VMEM((SC_GATHER_CHUNK,), jnp.int32)

    @functools.partial(
        pl.kernel, mesh=mesh,
        out_type=jax.ShapeDtypeStruct((TOP_K, n, *TOKEN_TILE), F32),
        scratch_types=[idx, idx, buf, buf, pltpu.SemaphoreType.DMA, pltpu.SemaphoreType.DMA],
        name="moe_gather_sc")
    def gather(y_hbm, d0_hbm, d1_hbm, out_hbm, idx0, idx1, rows0, rows1, sem0, sem1):
        worker = lax.axis_index("s") * SC_CORES + lax.axis_index("c")
        base = worker * per_worker
        for c in range(per_worker // SC_GATHER_CHUNK):
            t0 = base + c * SC_GATHER_CHUNK
            pltpu.sync_copy(d0_hbm.at[pl.ds(t0, SC_GATHER_CHUNK)], idx0)
            pltpu.sync_copy(d1_hbm.at[pl.ds(t0, SC_GATHER_CHUNK)], idx1)
            first = pltpu.async_copy(y_hbm.at[idx0], rows0, sem0)
            second = pltpu.async_copy(y_hbm.at[idx1], rows1, sem1)
            first.wait()
            pltpu.sync_copy(rows0, out_hbm.at[0, pl.ds(t0, SC_GATHER_CHUNK)])
            second.wait()
            pltpu.sync_copy(rows1, out_hbm.at[1, pl.ds(t0, SC_GATHER_CHUNK)])

    return gather(y, dest0, dest1)


N_STAGE = 3
WEIGHT_DMA_PRIORITY = 1
X_SLOTS = 4
Y_SLOTS = 4


def _expert_body(layer, blk_expert, blk_rows, next_expert, n_used, xs_hbm, wg_hbm, wu_hbm, wd_hbm,
                 y_hbm, xbuf, ybuf, sg, su, sd, wg, wu, wd, slot_ref, wsem, xsem, ysem):
    i = pl.program_id(0)
    e = blk_expert[i]
    nu = n_used[0]

    def x_copy(blk, slot):
        return pltpu.make_async_copy(xs_hbm.at[pl.ds(blk * ROW_BLOCK, ROW_BLOCK)], xbuf.at[slot],
                                     xsem.at[slot])

    def y_copy(blk, slot):
        return pltpu.make_async_copy(ybuf.at[slot], y_hbm.at[pl.ds(blk * ROW_BLOCK, ROW_BLOCK)],
                                     ysem.at[slot])

    def fetch(expert, slot):
        return (pltpu.make_async_copy(wg_hbm.at[layer, expert], sg.at[slot], wsem.at[slot, 0]),
                pltpu.make_async_copy(wu_hbm.at[layer, expert], su.at[slot], wsem.at[slot, 1]),
                pltpu.make_async_copy(wd_hbm.at[layer, expert], sd.at[slot], wsem.at[slot, 2]))

    def start_fetch(expert, slot):
        for c in fetch(expert, slot):
            c.start(priority=WEIGHT_DMA_PRIORITY)

    nxt = next_expert[e]
    nxt2 = next_expert[nxt]

    @pl.when(jnp.logical_and(i == 0, nu > 0))
    def _():
        slot_ref[0] = 0
        start_fetch(e, 0)

        @pl.when(nxt != e)
        def _():
            start_fetch(nxt, 1)

        for b in range(X_SLOTS - 1):
            @pl.when(b < nu)
            def _():
                x_copy(b, b).start()

    used = i < nu
    first_of_expert = jnp.logical_or(i == 0, blk_expert[jnp.maximum(i - 1, 0)] != e)
    xslot = i % X_SLOTS
    yslot = i % Y_SLOTS

    @pl.when(i >= Y_SLOTS)
    def _():
        y_copy(i - Y_SLOTS, yslot).wait()

    @pl.when(jnp.logical_and(used, first_of_expert))
    def _():
        slot = slot_ref[0]

        @pl.when(nxt2 != nxt)
        def _():
            start_fetch(nxt2, (slot + N_STAGE - 1) % N_STAGE)

        for c in fetch(e, slot):
            c.wait()
        wg[...] = sg[slot].astype(BF16)
        wu[...] = su[slot].astype(BF16)
        wd[...] = sd[slot].astype(BF16)
        slot_ref[0] = (slot + 1) % N_STAGE

    @pl.when(used)
    def _():
        ahead = i + X_SLOTS - 1

        @pl.when(ahead < nu)
        def _():
            x_copy(ahead, ahead % X_SLOTS).start()

        x_copy(i, xslot).wait()
        row = lax.broadcasted_iota(jnp.int32, (ROW_BLOCK, 1), 0)
        x = jnp.where(row < blk_rows[i], xbuf[xslot].reshape(ROW_BLOCK, D_MODEL), 0.0).astype(BF16)
        a = _dot(x, wg[...])
        b = _dot(x, wu[...])
        hmid = (a * (1.0 / (1.0 + jnp.exp(-a)))) * b
        ybuf[yslot] = _dot(hmid.astype(BF16), wd[...]).reshape(ROW_BLOCK, *TOKEN_TILE)

    @pl.when(jnp.logical_not(used))
    def _():
        ybuf[yslot] = jnp.zeros((ROW_BLOCK, *TOKEN_TILE), F32)

    y_copy(i, yslot).start()

    @pl.when(i == N_BLK - 1)
    def _():
        for s in range(Y_SLOTS):
            y_copy(0, s).wait()


def _experts(layer, blk_expert, blk_rows, next_expert, n_used, xs, w_gate, w_up, w_down):
    assert N_BLK >= Y_SLOTS
    hbm = pl.BlockSpec(memory_space=pl.ANY)
    return pl.pallas_call(
        functools.partial(_expert_body, layer),
        grid_spec=pltpu.PrefetchScalarGridSpec(
            num_scalar_prefetch=4,
            grid=(N_BLK,),
            in_specs=[hbm, hbm, hbm, hbm],
            out_specs=hbm,
            scratch_shapes=[
                pltpu.VMEM((X_SLOTS, ROW_BLOCK, *TOKEN_TILE), F32),
                pltpu.VMEM((Y_SLOTS, ROW_BLOCK, *TOKEN_TILE), F32),
                pltpu.VMEM((N_STAGE, D_MODEL, D_EXPERT), F32),
                pltpu.VMEM((N_STAGE, D_MODEL, D_EXPERT), F32),
                pltpu.VMEM((N_STAGE, D_EXPERT, D_MODEL), F32),
                pltpu.VMEM((D_MODEL, D_EXPERT), BF16),
                pltpu.VMEM((D_MODEL, D_EXPERT), BF16),
                pltpu.VMEM((D_EXPERT, D_MODEL), BF16),
                pltpu.SMEM((1,), jnp.int32),
                pltpu.SemaphoreType.DMA((N_STAGE, 3)),
                pltpu.SemaphoreType.DMA((X_SLOTS,)),
                pltpu.SemaphoreType.DMA((Y_SLOTS,)),
            ],
        ),
        out_shape=jax.ShapeDtypeStruct((N_ROWS, *TOKEN_TILE), F32),
        compiler_params=_params(("arbitrary",), 40),
        name="moe_experts",
    )(blk_expert, blk_rows, next_expert, n_used, xs, w_gate, w_up, w_down)


CB_TC = 512


def _combine_body(y0_ref, y1_ref, rg_ref, x_ref, fg_ref, out_ref):
    out_ref[...] = _rms(_add_moe(x_ref[...], y0_ref, y1_ref, rg_ref), fg_ref[...])


def _combine(y2, x, rg, final_g):
    n = x.shape[0]
    return pl.pallas_call(
        _combine_body,
        grid=(n // CB_TC,),
        in_specs=_moe_specs(CB_TC) + [
            pl.BlockSpec((CB_TC, D_MODEL), lambda i: (i, 0)),
            pl.BlockSpec((1, D_MODEL), lambda i: (0, 0)),
        ],
        out_specs=pl.BlockSpec((CB_TC, D_MODEL), lambda i: (i, 0)),
        out_shape=jax.ShapeDtypeStruct((n, D_MODEL), F32),
        compiler_params=_params(("arbitrary",), 32),
        name="moe_combine",
    )(y2, y2, rg, x, final_g)


def _moe(layer, x, norm_g, group_w, group_b, exp_w, exp_b, w_gate, w_up, w_down):
    pad_w = ROUTER_LANES - N_GROUPS - N_EXPERTS
    wr = jnp.concatenate([group_w, exp_w, jnp.zeros((D_MODEL, pad_w), F32)], axis=1)
    br = jnp.concatenate([group_b, exp_b, jnp.zeros((pad_w,), F32)])[None, :]
    wr_hi = wr.astype(BF16)
    wr_lo = (wr - wr_hi.astype(F32)).astype(BF16)
    h, ri, rg, cnt = _router(x, norm_g[None, :], jnp.stack([wr_hi, wr_lo]), br)

    experts = jnp.arange(N_EXPERTS, dtype=jnp.int32)
    counts = cnt[0, :N_EXPERTS].astype(jnp.int32)
    pcounts = (counts + ROW_BLOCK - 1) // ROW_BLOCK * ROW_BLOCK
    pends = jnp.cumsum(pcounts)
    pstarts = pends - pcounts
    dest0, dest1 = (
        (jnp.sum(jnp.where(ri[:, s, None] == experts, pstarts, 0), axis=-1) + ri[:, TOP_K + s]).astype(jnp.int32)
        for s in range(TOP_K))
    blk_row0 = jnp.arange(N_BLK, dtype=jnp.int32) * ROW_BLOCK
    blk_expert = jnp.minimum(jnp.sum(pends[None, :] <= blk_row0[:, None], axis=1),
                             N_EXPERTS - 1).astype(jnp.int32)
    n_used = (pends[-1:] // ROW_BLOCK).astype(jnp.int32)
    later_used = jnp.logical_and(experts[None, :] > experts[:, None], (pcounts > 0)[None, :])
    next_expert = jnp.min(jnp.where(later_used, experts[None, :], N_EXPERTS), axis=1)
    next_expert = jnp.where(next_expert == N_EXPERTS, experts, next_expert).astype(jnp.int32)

    owner = blk_expert[:, None] == experts[None, :]
    blk_rows = jnp.sum(jnp.where(owner, counts + pstarts, 0), axis=1) - blk_row0
    blk_rows = jnp.where(blk_row0 < pends[-1], jnp.clip(blk_rows, 0, ROW_BLOCK), 0).astype(jnp.int32)

    xs = _sc_dispatch(h, dest0, dest1)
    y = _experts(layer, blk_expert, blk_rows, next_expert, n_used, xs, w_gate, w_up, w_down)
    return _sc_gather(y, dest0, dest1), rg


def kernel(x, norm_mix, norm_ffn, norm_final, ab_w_in, ab_w_out, pool_w, pool_scale, gm_w_in, gm_norm, gm_ws, gm_bs, gm_w_out, router_group_w, router_group_b, router_expert_w, router_expert_b, moe_w_gate, moe_w_up, moe_w_down):
    assert x.shape == (BATCH, SEQ, D_MODEL) and x.dtype == F32
    xf = x.reshape(N_TOK, D_MODEL)
    moe = None
    for layer in range(DEPTH):
        i = layer // 2
        if layer % 2 == 0:
            xf, zp, qkv = _ab_in(xf, moe, norm_mix[layer][None, :], ab_w_in[i].astype(BF16))
            o = _attention(qkv.reshape(BATCH, SEQ, 3 * ATTN_WIDTH))
            xf = _ab_out(zp.reshape(BATCH, SEQ, POOL_WIDTH), o, xf.reshape(BATCH, SEQ, D_MODEL),
                         pool_w[i].astype(BF16), pool_scale[i][None, :],
                         ab_w_out[i].astype(BF16)).reshape(N_TOK, D_MODEL)
        else:
            bs_bcast = jnp.broadcast_to(gm_bs[i][:, :, None], (GM_GROUPS, GM_CHUNK, GM_CHUNK))
            xf = _gmlp(xf, moe, norm_mix[layer][None, :], gm_w_in[i].astype(BF16), gm_norm[i][None, :],
                       gm_ws[i], bs_bcast, gm_w_out[i].astype(BF16))
        moe = _moe(layer, xf, norm_ffn[layer], router_group_w[layer], router_group_b[layer],
                   router_expert_w[layer], router_expert_b[layer], moe_w_gate, moe_w_up, moe_w_down)
    return _combine(moe[0], xf, moe[1], norm_final[None, :]).reshape(BATCH, SEQ, D_MODEL)
```

```python
import functools

import jax
import jax.numpy as jnp
from jax import lax
from jax.experimental import pallas as pl
from jax.experimental.pallas import tpu as pltpu
from jax.experimental.pallas import tpu_sc as plsc

D_MODEL = 1024
BATCH = 8
SEQ = 2048
DEPTH = 4
N_TOK = BATCH * SEQ

POOL_WINDOWS = (2, 4, 8, 16)
POOL_CH = 128
POOL_WIDTH = 512
POOL_HALO = 16
HEAD_DIM = 64
ATTN_WIDTH = 512
HEADS_PER_STEP = 2
ATTN_BLOCK = 128
GM_CHUNK = 128
GM_GROUPS = 8
GM_WIDTH = 1024
N_GROUPS = 8
N_EXPERTS = 64
TOP_K = 2
D_EXPERT = 512
ROW_BLOCK = 128
N_ROWS = (N_TOK * TOP_K + N_EXPERTS * (ROW_BLOCK - 1) + ROW_BLOCK - 1) // ROW_BLOCK * ROW_BLOCK
N_BLK = N_ROWS // ROW_BLOCK
EPS = 1e-6
NEG = -1e30

LANES = 128
SUBLANES = 8
TOKEN_TILE = (SUBLANES, LANES)
assert SUBLANES * LANES == D_MODEL
V7X_VMEM_BYTES = 64 * 1024 * 1024
MIB = 1024 * 1024

F32 = jnp.float32
BF16 = jnp.bfloat16


def _params(semantics, vmem_mib):
    assert vmem_mib * MIB < V7X_VMEM_BYTES
    return pltpu.CompilerParams(dimension_semantics=semantics, vmem_limit_bytes=vmem_mib * MIB)


def _rms(x, g):
    return x * lax.rsqrt(jnp.mean(x * x, axis=-1, keepdims=True) + EPS) * g


def _dot(a, b):
    return jnp.dot(a, b, preferred_element_type=F32)


def _dot_nt(a, b):
    return lax.dot_general(a, b, (((1,), (1,)), ((), ())), preferred_element_type=F32)


AB_TM = 512


def _moe_specs(rows):
    return [
        pl.BlockSpec((None, rows, *TOKEN_TILE), lambda i: (0, i, 0, 0)),
        pl.BlockSpec((None, rows, *TOKEN_TILE), lambda i: (1, i, 0, 0)),
        pl.BlockSpec((rows, ROUTER_LANES), lambda i: (i, 0)),
    ]


def _add_moe(x, y0_ref, y1_ref, rg_ref):
    rows = x.shape[0]
    rg = rg_ref[...]
    return (x + rg[:, 0:1] * y0_ref[...].reshape(rows, D_MODEL)
            + rg[:, 1:2] * y1_ref[...].reshape(rows, D_MODEL))


def _ab_in_body(has_moe, *refs):
    if has_moe:
        y0_ref, y1_ref, rg_ref, x_ref, g_ref, w_ref, xnew_ref, zp_ref, qkv_ref = refs
        x = _add_moe(x_ref[...], y0_ref, y1_ref, rg_ref)
        xnew_ref[...] = x
    else:
        x_ref, g_ref, w_ref, zp_ref, qkv_ref = refs
        x = x_ref[...]
    h = _rms(x, g_ref[...]).astype(BF16)
    z = _dot(h, w_ref[...])
    zp_ref[...] = z[:, :POOL_WIDTH]
    q = z[:, POOL_WIDTH:POOL_WIDTH + ATTN_WIDTH] * (HEAD_DIM ** -0.5)
    qkv_ref[...] = jnp.concatenate([q, z[:, POOL_WIDTH + ATTN_WIDTH:]], axis=1).astype(BF16)


def _ab_in(x, moe, g, w_bf16):
    n = x.shape[0]
    has_moe = moe is not None
    row_spec = pl.BlockSpec((AB_TM, D_MODEL), lambda i: (i, 0))
    outs = pl.pallas_call(
        functools.partial(_ab_in_body, has_moe),
        grid=(n // AB_TM,),
        in_specs=(_moe_specs(AB_TM) if has_moe else []) + [
            row_spec,
            pl.BlockSpec((1, D_MODEL), lambda i: (0, 0)),
            pl.BlockSpec((D_MODEL, 4 * ATTN_WIDTH), lambda i: (0, 0)),
        ],
        out_specs=([row_spec] if has_moe else []) + [
            pl.BlockSpec((AB_TM, POOL_WIDTH), lambda i: (i, 0)),
            pl.BlockSpec((AB_TM, 3 * ATTN_WIDTH), lambda i: (i, 0)),
        ],
        out_shape=([jax.ShapeDtypeStruct((n, D_MODEL), F32)] if has_moe else []) + [
            jax.ShapeDtypeStruct((n, POOL_WIDTH), F32),
            jax.ShapeDtypeStruct((n, 3 * ATTN_WIDTH), BF16),
        ],
        compiler_params=_params(("arbitrary",), 48),
        name="ab_in",
    )(*((moe[0], moe[0], moe[1]) if has_moe else ()), x, g, w_bf16)
    return tuple(outs) if has_moe else (x, *outs)


ATTN_UNROLL = 16
MERGE_UNROLL = 4


def _attn_body(q_ref, k_ref, v_ref, o_ref, qf, kf, vf, o1, o4, o16, l1, l4, l16):
    qf[...] = q_ref[...].astype(F32)
    kf[...] = k_ref[...].astype(F32)
    vf[...] = v_ref[...].astype(F32)

    key = lax.broadcasted_iota(jnp.int32, (ATTN_BLOCK, HEADS_PER_STEP * ATTN_BLOCK), 0)
    qry = lax.broadcasted_iota(jnp.int32, (ATTN_BLOCK, HEADS_PER_STEP * ATTN_BLOCK), 1) % ATTN_BLOCK
    cur_ok = key <= qry
    prev_ok = key >= qry
    head0_lane = lax.broadcasted_iota(jnp.int32, (ATTN_BLOCK, LANES), 1) < HEAD_DIM

    def rows(ref, start, d):
        if d == 1:
            return ref[pl.ds(start, ATTN_BLOCK), :]
        return ref[pl.ds(start, ATTN_BLOCK, stride=d), :]

    def put(ref, start, d, val):
        if d == 1:
            ref[pl.ds(start, ATTN_BLOCK), :] = val
        else:
            ref[pl.ds(start, ATTN_BLOCK, stride=d), :] = val

    for d, o_acc, l_acc in ((1, o1, l1), (4, o4, l4), (16, o16, l16)):
        nb = SEQ // d // ATTN_BLOCK
        has_prev = nb > 1

        n_blocks = SEQ // ATTN_BLOCK
        halves = [slice(hh * ATTN_BLOCK, (hh + 1) * ATTN_BLOCK) for hh in range(HEADS_PER_STEP)]
        chans = [slice(hh * HEAD_DIM, (hh + 1) * HEAD_DIM) for hh in range(HEADS_PER_STEP)]

        def block_start(idx, d=d, nb=nb):
            return idx // nb + (idx % nb) * (ATTN_BLOCK * d)

        def scores(idx, k_prev, d=d, has_prev=has_prev):
            start = block_start(idx)
            qv = rows(qf, start, d)
            q2 = jnp.concatenate([jnp.where(head0_lane, qv, 0.0), jnp.where(head0_lane, 0.0, qv)],
                                 axis=0).astype(BF16)
            kc = rows(kf, start, d).astype(BF16)
            keys = jnp.concatenate([k_prev, kc], axis=0) if has_prev else kc
            return _dot_nt(keys, q2), kc

        def finish(pv, den, lse, start, d=d, o_acc=o_acc, l_acc=l_acc):
            out_t = jnp.concatenate([pv[c, h] / den[:, h] for c, h in zip(chans, halves)], axis=0)
            lse_t = jnp.concatenate([jnp.broadcast_to(lse[:, h], (HEAD_DIM, ATTN_BLOCK)) for h in halves],
                                    axis=0)
            put(o_acc, start, d, out_t.T)
            put(l_acc, start, d, lse_t.T)

        def block(idx, carry, d=d, nb=nb, has_prev=has_prev):
            s_raw, k_cur, v_prev_t, pv_last, den_last, lse_last, start_last = carry
            s_next, k_next = scores(jnp.minimum(idx + 1, n_blocks - 1), k_cur)
            finish(pv_last, den_last, lse_last, start_last)
            start = block_start(idx)
            vc_t = rows(vf, start, d).T.astype(BF16)
            if has_prev:
                vals_t = jnp.concatenate([v_prev_t, vc_t], axis=1)
                ok = jnp.concatenate([jnp.logical_and(prev_ok, idx % nb > 0), cur_ok], axis=0)
            else:
                vals_t, ok = vc_t, cur_ok
            s = jnp.where(ok, s_raw, NEG)
            m = jnp.max(s, axis=0, keepdims=True)
            p = jnp.exp(s - m)
            den = jnp.sum(p, axis=0, keepdims=True)
            pv = _dot(vals_t, p.astype(BF16))
            return s_next, k_next, vc_t, pv, den, m + jnp.log(den), start

        k_none = jnp.zeros((ATTN_BLOCK, LANES), BF16)
        s_first, k_first = scores(0, k_none)
        width = HEADS_PER_STEP * ATTN_BLOCK
        init = (s_first, k_first, jnp.zeros((LANES, ATTN_BLOCK), BF16), jnp.zeros((LANES, width), F32),
                jnp.ones((1, width), F32), jnp.zeros((1, width), F32), jnp.int32(0))
        last = lax.fori_loop(0, n_blocks, block, init, unroll=ATTN_UNROLL)
        finish(*last[3:])

    def merge(c, carry):
        sl = pl.ds(pl.multiple_of(c * ATTN_BLOCK, ATTN_BLOCK), ATTN_BLOCK)
        la, lb, lc = l1[sl, :], l4[sl, :], l16[sl, :]
        mx = jnp.maximum(jnp.maximum(la, lb), lc)
        wa, wb, wc = jnp.exp(la - mx), jnp.exp(lb - mx), jnp.exp(lc - mx)
        num = wa * o1[sl, :] + wb * o4[sl, :] + wc * o16[sl, :]
        o_ref[sl, :] = (num / (wa + wb + wc)).astype(o_ref.dtype)
        return carry

    lax.fori_loop(0, SEQ // ATTN_BLOCK, merge, 0, unroll=MERGE_UNROLL)


def _attention(qkv):
    n_hp = ATTN_WIDTH // LANES
    blk = (None, SEQ, LANES)
    return pl.pallas_call(
        _attn_body,
        grid=(qkv.shape[0], n_hp),
        in_specs=[
            pl.BlockSpec(blk, lambda b, hp: (b, 0, hp)),
            pl.BlockSpec(blk, lambda b, hp: (b, 0, n_hp + hp)),
            pl.BlockSpec(blk, lambda b, hp: (b, 0, 2 * n_hp + hp)),
        ],
        out_specs=pl.BlockSpec(blk, lambda b, hp: (b, 0, hp)),
        out_shape=jax.ShapeDtypeStruct((qkv.shape[0], SEQ, ATTN_WIDTH), BF16),
        scratch_shapes=[pltpu.VMEM((SEQ, LANES), F32) for _ in range(9)],
        compiler_params=_params(("arbitrary", "arbitrary"), 32),
        name="dilated_attn",
    )(qkv, qkv, qkv)


AO_TS = 512


def _ab_out_body(zp_ref, halo_ref, o_ref, x_ref, pw_ref, ps_ref, w_ref, out_ref, zz):
    i = pl.program_id(1)
    zz[0:POOL_HALO, :] = jnp.where(i > 0, halo_ref[...], 0.0)
    zz[POOL_HALO:, :] = zp_ref[...]
    pos = i * AO_TS + lax.broadcasted_iota(jnp.int32, (AO_TS, 1), 0)
    parts = []
    for g, w in enumerate(POOL_WINDOWS):
        cols = slice(g * POOL_CH, (g + 1) * POOL_CH)
        acc = zz[pl.ds(POOL_HALO, AO_TS), cols]
        for j in range(1, w):
            acc = acc + zz[pl.ds(POOL_HALO - j, AO_TS), cols]
        count = jnp.minimum(pos + 1, w).astype(F32)
        pooled = acc / count - zp_ref[:, cols]
        parts.append(_dot(pooled.astype(BF16), pw_ref[g]))
    a = jnp.concatenate(parts, axis=1) * ps_ref[...]
    y = jnp.concatenate([a.astype(BF16), o_ref[...]], axis=1)
    out_ref[...] = x_ref[...] + _dot(y, w_ref[...])


def _ab_out(zp, o, x, pool_w_bf16, pool_scale, w_out_bf16):
    halo_per_tile = AO_TS // POOL_HALO
    return pl.pallas_call(
        _ab_out_body,
        grid=(BATCH, SEQ // AO_TS),
        in_specs=[
            pl.BlockSpec((None, AO_TS, POOL_WIDTH), lambda b, i: (b, i, 0)),
            pl.BlockSpec((None, POOL_HALO, POOL_WIDTH),
                         lambda b, i: (b, jnp.maximum(i * halo_per_tile - 1, 0), 0)),
            pl.BlockSpec((None, AO_TS, ATTN_WIDTH), lambda b, i: (b, i, 0)),
            pl.BlockSpec((None, AO_TS, D_MODEL), lambda b, i: (b, i, 0)),
            pl.BlockSpec((len(POOL_WINDOWS), POOL_CH, POOL_CH), lambda b, i: (0, 0, 0)),
            pl.BlockSpec((1, POOL_WIDTH), lambda b, i: (0, 0)),
            pl.BlockSpec((POOL_WIDTH + ATTN_WIDTH, D_MODEL), lambda b, i: (0, 0)),
        ],
        out_specs=pl.BlockSpec((None, AO_TS, D_MODEL), lambda b, i: (b, i, 0)),
        out_shape=jax.ShapeDtypeStruct((BATCH, SEQ, D_MODEL), F32),
        scratch_shapes=[pltpu.VMEM((AO_TS + POOL_HALO, POOL_WIDTH), F32)],
        compiler_params=_params(("arbitrary", "arbitrary"), 32),
        name="pool_ab_out",
    )(zp, zp, o, x, pool_w_bf16, pool_scale, w_out_bf16)


GM_TS = 512
GM_SUB = 256


GELU_C = 0.7978845608028654
GELU_A = 0.044715


def _gelu_tanh(x):
    return x * (0.5 + 0.5 * jnp.tanh(x * (GELU_C + (GELU_C * GELU_A) * (x * x))))


def _gmlp_body(y0_ref, y1_ref, rg_ref, x_ref, g_ref, win_ref, gn_ref, ws_ref, bs_ref, wout_ref, out_ref):
    x_all = _add_moe(x_ref[...], y0_ref, y1_ref, rg_ref)
    ri = lax.broadcasted_iota(jnp.int32, (GM_CHUNK, GM_CHUNK), 0)
    ci = lax.broadcasted_iota(jnp.int32, (GM_CHUNK, GM_CHUNK), 1)
    causal = ci <= ri
    ws_c = [jnp.where(causal, ws_ref[g], 0.0).astype(BF16) for g in range(GM_GROUPS)]
    for part in range(GM_TS // GM_SUB):
        psl = slice(part * GM_SUB, (part + 1) * GM_SUB)
        x = x_all[psl]
        h = _rms(x, g_ref[...]).astype(BF16)
        z = _gelu_tanh(_dot(h, win_ref[...]))
        u = z[:, :GM_WIDTH]
        v = z[:, GM_WIDTH:]
        vc = v - jnp.mean(v, axis=-1, keepdims=True)
        vn = (vc * lax.rsqrt(jnp.mean(vc * vc, axis=-1, keepdims=True) + EPS) * gn_ref[...]).astype(BF16)
        rows = []
        for c in range(GM_SUB // GM_CHUNK):
            rsl = slice(c * GM_CHUNK, (c + 1) * GM_CHUNK)
            cols = []
            for g in range(GM_GROUPS):
                csl = slice(g * (GM_WIDTH // GM_GROUPS), (g + 1) * (GM_WIDTH // GM_GROUPS))
                sv = _dot(ws_c[g], vn[rsl, csl]) + bs_ref[g]
                cols.append(u[rsl, csl] * sv)
            rows.append(jnp.concatenate(cols, axis=1))
        y = jnp.concatenate(rows, axis=0).astype(BF16)
        out_ref[psl, :] = x + _dot(y, wout_ref[...])


def _gmlp(x, moe, g, w_in_bf16, gm_norm, ws, bs_bcast, w_out_bf16):
    n = x.shape[0]
    const2 = lambda i: (0, 0)
    const3 = lambda i: (0, 0, 0)
    return pl.pallas_call(
        _gmlp_body,
        grid=(n // GM_TS,),
        in_specs=_moe_specs(GM_TS) + [
            pl.BlockSpec((GM_TS, D_MODEL), lambda i: (i, 0)),
            pl.BlockSpec((1, D_MODEL), const2),
            pl.BlockSpec((D_MODEL, 2 * GM_WIDTH), const2),
            pl.BlockSpec((1, GM_WIDTH), const2),
            pl.BlockSpec((GM_GROUPS, GM_CHUNK, GM_CHUNK), const3),
            pl.BlockSpec((GM_GROUPS, GM_CHUNK, GM_CHUNK), const3),
            pl.BlockSpec((GM_WIDTH, D_MODEL), const2),
        ],
        out_specs=pl.BlockSpec((GM_TS, D_MODEL), lambda i: (i, 0)),
        out_shape=jax.ShapeDtypeStruct((n, D_MODEL), F32),
        compiler_params=_params(("arbitrary",), 40),
        name="gmlp",
    )(moe[0], moe[0], moe[1], x, g, w_in_bf16, gm_norm, ws, bs_bcast, w_out_bf16)


RT_TN = 512
ROUTER_LANES = LANES
EXPERT_LANE0 = N_GROUPS


def _router_body(x_ref, g_ref, wr_ref, br_ref, h_ref, ri_ref, rg_ref, cnt_ref, tri, running):
    step = pl.program_id(0)

    @pl.when(step == 0)
    def _():
        r = lax.broadcasted_iota(jnp.int32, (RT_TN, RT_TN), 0)
        c = lax.broadcasted_iota(jnp.int32, (RT_TN, RT_TN), 1)
        tri[...] = jnp.where(c < r, 1.0, 0.0).astype(BF16)
        running[...] = jnp.zeros_like(running)

    h = _rms(x_ref[...], g_ref[...])
    h_ref[...] = h.reshape(RT_TN, *TOKEN_TILE)
    h_hi = h.astype(BF16)
    h_lo = (h - h_hi.astype(F32)).astype(BF16)
    logits = (_dot(h_hi, wr_ref[0]) + (_dot(h_hi, wr_ref[1]) + _dot(h_lo, wr_ref[0]))) + br_ref[...]
    lane = lax.broadcasted_iota(jnp.int32, (RT_TN, ROUTER_LANES), 1)

    is_group = lane < N_GROUPS
    lg = jnp.where(is_group, logits, NEG)
    mg = jnp.max(lg, axis=1, keepdims=True)
    p_top = 1.0 / jnp.sum(jnp.where(is_group, jnp.exp(logits - mg), 0.0), axis=1, keepdims=True)
    g_idx = jnp.min(jnp.where(lg == mg, lane, ROUTER_LANES), axis=1, keepdims=True)

    in_group = jnp.logical_and(
        jnp.logical_and(lane >= EXPERT_LANE0, lane < EXPERT_LANE0 + N_EXPERTS),
        ((lane - EXPERT_LANE0) >> 3) == g_idx)
    le = jnp.where(in_group, logits, NEG)
    t1 = jnp.max(le, axis=1, keepdims=True)
    i1 = jnp.min(jnp.where(le == t1, lane, ROUTER_LANES), axis=1, keepdims=True)
    le2 = jnp.where(lane == i1, NEG, le)
    t2 = jnp.max(le2, axis=1, keepdims=True)
    i2 = jnp.min(jnp.where(le2 == t2, lane, ROUTER_LANES), axis=1, keepdims=True)
    e0 = i1 - EXPERT_LANE0
    e1 = i2 - EXPERT_LANE0
    ex = jnp.exp(t2 - t1)
    w0 = 1.0 / (1.0 + ex)
    g0 = p_top * w0
    g1 = p_top * (ex * w0)

    oh0 = lane == e0
    oh1 = lane == e1
    cnt = jnp.where(jnp.logical_or(oh0, oh1), 1.0, 0.0)
    base = running[...] + _dot(tri[...], cnt.astype(BF16))
    rank0 = jnp.sum(jnp.where(oh0, base, 0.0), axis=1, keepdims=True).astype(jnp.int32)
    rank1 = jnp.sum(jnp.where(oh1, base, 0.0), axis=1, keepdims=True).astype(jnp.int32)
    running[...] = running[...] + jnp.sum(cnt, axis=0, keepdims=True)
    cnt_ref[...] = running[...]

    ri_ref[...] = jnp.where(lane == 0, e0, jnp.where(lane == 1, e1,
                            jnp.where(lane == 2, rank0, jnp.where(lane == 3, rank1, 0))))
    rg_ref[...] = jnp.where(lane == 0, g0, jnp.where(lane == 1, g1, 0.0))


def _router(x, g, wr, br):
    n = x.shape[0]
    const2 = lambda i: (0, 0)
    return pl.pallas_call(
        _router_body,
        grid=(n // RT_TN,),
        in_specs=[
            pl.BlockSpec((RT_TN, D_MODEL), lambda i: (i, 0)),
            pl.BlockSpec((1, D_MODEL), const2),
            pl.BlockSpec((2, D_MODEL, ROUTER_LANES), lambda i: (0, 0, 0)),
            pl.BlockSpec((1, ROUTER_LANES), const2),
        ],
        out_specs=[
            pl.BlockSpec((RT_TN, *TOKEN_TILE), lambda i: (i, 0, 0)),
            pl.BlockSpec((RT_TN, ROUTER_LANES), lambda i: (i, 0)),
            pl.BlockSpec((RT_TN, ROUTER_LANES), lambda i: (i, 0)),
            pl.BlockSpec((1, ROUTER_LANES), const2),
        ],
        out_shape=[
            jax.ShapeDtypeStruct((n, *TOKEN_TILE), F32),
            jax.ShapeDtypeStruct((n, ROUTER_LANES), jnp.int32),
            jax.ShapeDtypeStruct((n, ROUTER_LANES), F32),
            jax.ShapeDtypeStruct((1, ROUTER_LANES), F32),
        ],
        scratch_shapes=[pltpu.VMEM((RT_TN, RT_TN), BF16), pltpu.VMEM((1, ROUTER_LANES), F32)],
        compiler_params=_params(("arbitrary",), 32),
        name="router",
    )(x, g, wr, br)


SC_CORES = 2
SC_SUBCORES = 16
SC_WORKERS = SC_CORES * SC_SUBCORES
SC_CHUNK = 64


def _sc_dispatch(h, dest0, dest1):
    n = h.shape[0]
    per_worker = n // SC_WORKERS
    assert per_worker % SC_CHUNK == 0
    mesh = plsc.VectorSubcoreMesh(core_axis_name="c", subcore_axis_name="s")

    @functools.partial(
        pl.kernel, mesh=mesh,
        out_type=jax.ShapeDtypeStruct((N_ROWS, *TOKEN_TILE), F32),
        scratch_types=[pltpu.VMEM((SC_CHUNK,), jnp.int32), pltpu.VMEM((SC_CHUNK,), jnp.int32),
                       pltpu.VMEM((SC_CHUNK, *TOKEN_TILE), F32), pltpu.SemaphoreType.DMA],
        name="moe_dispatch_sc")
    def scatter(h_hbm, d0_hbm, d1_hbm, xs_hbm, idx0, idx1, rows, sem):
        worker = lax.axis_index("s") * SC_CORES + lax.axis_index("c")
        base = worker * per_worker
        for c in range(per_worker // SC_CHUNK):
            t0 = base + c * SC_CHUNK
            pltpu.sync_copy(d0_hbm.at[pl.ds(t0, SC_CHUNK)], idx0)
            pltpu.sync_copy(d1_hbm.at[pl.ds(t0, SC_CHUNK)], idx1)
            pltpu.sync_copy(h_hbm.at[pl.ds(t0, SC_CHUNK)], rows)
            first = pltpu.async_copy(rows, xs_hbm.at[idx0], sem)
            second = pltpu.async_copy(rows, xs_hbm.at[idx1], sem)
            first.wait()
            second.wait()

    return scatter(h, dest0, dest1)


SC_GATHER_CHUNK = 32


def _sc_gather(y, dest0, dest1):
    n = dest0.shape[0]
    per_worker = n // SC_WORKERS
    assert per_worker % SC_GATHER_CHUNK == 0
    mesh = plsc.VectorSubcoreMesh(core_axis_name="c", subcore_axis_name="s")
    buf = pltpu.VMEM((SC_GATHER_CHUNK, *TOKEN_TILE), F32)
    idx = pltpu.VMEM((SC_GATHER_CHUNK,), jnp.int32)

    @functools.partial(
        pl.kernel, mesh=mesh,
        out_type=jax.ShapeDtypeStruct((TOP_K, n, *TOKEN_TILE), F32),
        scratch_types=[idx, idx, buf, buf, pltpu.SemaphoreType.DMA, pltpu.SemaphoreType.DMA],
        name="moe_gather_sc")
    def gather(y_hbm, d0_hbm, d1_hbm, out_hbm, idx0, idx1, rows0, rows1, sem0, sem1):
        worker = lax.axis_index("s") * SC_CORES + lax.axis_index("c")
        base = worker * per_worker
        for c in range(per_worker // SC_GATHER_CHUNK):
            t0 = base + c * SC_GATHER_CHUNK
            pltpu.sync_copy(d0_hbm.at[pl.ds(t0, SC_GATHER_CHUNK)], idx0)
            pltpu.sync_copy(d1_hbm.at[pl.ds(t0, SC_GATHER_CHUNK)], idx1)
            first = pltpu.async_copy(y_hbm.at[idx0], rows0, sem0)
            second = pltpu.async_copy(y_hbm.at[idx1], rows1, sem1)
            first.wait()
            pltpu.sync_copy(rows0, out_hbm.at[0, pl.ds(t0, SC_GATHER_CHUNK)])
            second.wait()
            pltpu.sync_copy(rows1, out_hbm.at[1, pl.ds(t0, SC_GATHER_CHUNK)])

    return gather(y, dest0, dest1)


N_STAGE = 3
WEIGHT_DMA_PRIORITY = 1
X_SLOTS = 4
Y_SLOTS = 4


def _expert_body(layer, blk_expert, blk_rows, next_expert, n_used, xs_hbm, wg_hbm, wu_hbm, wd_hbm,
                 y_hbm, xbuf, ybuf, sg, su, sd, wg, wu, wd, slot_ref, wsem, xsem, ysem):
    i = pl.program_id(0)
    e = blk_expert[i]
    nu = n_used[0]

    def x_copy(blk, slot):
        return pltpu.make_async_copy(xs_hbm.at[pl.ds(blk * ROW_BLOCK, ROW_BLOCK)], xbuf.at[slot],
                                     xsem.at[slot])

    def y_copy(blk, slot):
        return pltpu.make_async_copy(ybuf.at[slot], y_hbm.at[pl.ds(blk * ROW_BLOCK, ROW_BLOCK)],
                                     ysem.at[slot])

    def fetch(expert, slot):
        return (pltpu.make_async_copy(wg_hbm.at[layer, expert], sg.at[slot], wsem.at[slot, 0]),
                pltpu.make_async_copy(wu_hbm.at[layer, expert], su.at[slot], wsem.at[slot, 1]),
                pltpu.make_async_copy(wd_hbm.at[layer, expert], sd.at[slot], wsem.at[slot, 2]))

    def start_fetch(expert, slot):
        for c in fetch(expert, slot):
            c.start(priority=WEIGHT_DMA_PRIORITY)

    nxt = next_expert[e]
    nxt2 = next_expert[nxt]

    @pl.when(jnp.logical_and(i == 0, nu > 0))
    def _():
        slot_ref[0] = 0
        start_fetch(e, 0)

        @pl.when(nxt != e)
        def _():
            start_fetch(nxt, 1)

        for b in range(X_SLOTS - 1):
            @pl.when(b < nu)
            def _():
                x_copy(b, b).start()

    used = i < nu
    first_of_expert = jnp.logical_or(i == 0, blk_expert[jnp.maximum(i - 1, 0)] != e)
    xslot = i % X_SLOTS
    yslot = i % Y_SLOTS

    @pl.when(i >= Y_SLOTS)
    def _():
        y_copy(i - Y_SLOTS, yslot).wait()

    @pl.when(jnp.logical_and(used, first_of_expert))
    def _():
        slot = slot_ref[0]

        @pl.when(nxt2 != nxt)
        def _():
            start_fetch(nxt2, (slot + N_STAGE - 1) % N_STAGE)

        for c in fetch(e, slot):
            c.wait()
        wg[...] = sg[slot].astype(BF16)
        wu[...] = su[slot].astype(BF16)
        wd[...] = sd[slot].astype(BF16)
        slot_ref[0] = (slot + 1) % N_STAGE

    @pl.when(used)
    def _():
        ahead = i + X_SLOTS - 1

        @pl.when(ahead < nu)
        def _():
            x_copy(ahead, ahead % X_SLOTS).start()

        x_copy(i, xslot).wait()
        row = lax.broadcasted_iota(jnp.int32, (ROW_BLOCK, 1), 0)
        x = jnp.where(row < blk_rows[i], xbuf[xslot].reshape(ROW_BLOCK, D_MODEL), 0.0).astype(BF16)
        a = _dot(x, wg[...])
        b = _dot(x, wu[...])
        hmid = (a * (1.0 / (1.0 + jnp.exp(-a)))) * b
        ybuf[yslot] = _dot(hmid.astype(BF16), wd[...]).reshape(ROW_BLOCK, *TOKEN_TILE)

    @pl.when(jnp.logical_not(used))
    def _():
        ybuf[yslot] = jnp.zeros((ROW_BLOCK, *TOKEN_TILE), F32)

    y_copy(i, yslot).start()

    @pl.when(i == N_BLK - 1)
    def _():
        for s in range(Y_SLOTS):
            y_copy(0, s).wait()


def _experts(layer, blk_expert, blk_rows, next_expert, n_used, xs, w_gate, w_up, w_down):
    assert N_BLK >= Y_SLOTS
    hbm = pl.BlockSpec(memory_space=pl.ANY)
    return pl.pallas_call(
        functools.partial(_expert_body, layer),
        grid_spec=pltpu.PrefetchScalarGridSpec(
            num_scalar_prefetch=4,
            grid=(N_BLK,),
            in_specs=[hbm, hbm, hbm, hbm],
            out_specs=hbm,
            scratch_shapes=[
                pltpu.VMEM((X_SLOTS, ROW_BLOCK, *TOKEN_TILE), F32),
                pltpu.VMEM((Y_SLOTS, ROW_BLOCK, *TOKEN_TILE), F32),
                pltpu.VMEM((N_STAGE, D_MODEL, D_EXPERT), F32),
                pltpu.VMEM((N_STAGE, D_MODEL, D_EXPERT), F32),
                pltpu.VMEM((N_STAGE, D_EXPERT, D_MODEL), F32),
                pltpu.VMEM((D_MODEL, D_EXPERT), BF16),
                pltpu.VMEM((D_MODEL, D_EXPERT), BF16),
                pltpu.VMEM((D_EXPERT, D_MODEL), BF16),
                pltpu.SMEM((1,), jnp.int32),
                pltpu.SemaphoreType.DMA((N_STAGE, 3)),
                pltpu.SemaphoreType.DMA((X_SLOTS,)),
                pltpu.SemaphoreType.DMA((Y_SLOTS,)),
            ],
        ),
        out_shape=jax.ShapeDtypeStruct((N_ROWS, *TOKEN_TILE), F32),
        compiler_params=_params(("arbitrary",), 40),
        name="moe_experts",
    )(blk_expert, blk_rows, next_expert, n_used, xs, w_gate, w_up, w_down)


CB_TC = 512


def _combine_body(y0_ref, y1_ref, rg_ref, x_ref, fg_ref, out_ref):
    out_ref[...] = _rms(_add_moe(x_ref[...], y0_ref, y1_ref, rg_ref), fg_ref[...])


def _combine(y2, x, rg, final_g):
    n = x.shape[0]
    return pl.pallas_call(
        _combine_body,
        grid=(n // CB_TC,),
        in_specs=_moe_specs(CB_TC) + [
            pl.BlockSpec((CB_TC, D_MODEL), lambda i: (i, 0)),
            pl.BlockSpec((1, D_MODEL), lambda i: (0, 0)),
        ],
        out_specs=pl.BlockSpec((CB_TC, D_MODEL), lambda i: (i, 0)),
        out_shape=jax.ShapeDtypeStruct((n, D_MODEL), F32),
        compiler_params=_params(("arbitrary",), 32),
        name="moe_combine",
    )(y2, y2, rg, x, final_g)


def _moe(layer, x, norm_g, group_w, group_b, exp_w, exp_b, w_gate, w_up, w_down):
    pad_w = ROUTER_LANES - N_GROUPS - N_EXPERTS
    wr = jnp.concatenate([group_w, exp_w, jnp.zeros((D_MODEL, pad_w), F32)], axis=1)
    br = jnp.concatenate([group_b, exp_b, jnp.zeros((pad_w,), F32)])[None, :]
    wr_hi = wr.astype(BF16)
    wr_lo = (wr - wr_hi.astype(F32)).astype(BF16)
    h, ri, rg, cnt = _router(x, norm_g[None, :], jnp.stack([wr_hi, wr_lo]), br)

    experts = jnp.arange(N_EXPERTS, dtype=jnp.int32)
    counts = cnt[0, :N_EXPERTS].astype(jnp.int32)
    pcounts = (counts + ROW_BLOCK - 1) // ROW_BLOCK * ROW_BLOCK
    pends = jnp.cumsum(pcounts)
    pstarts = pends - pcounts
    dest0, dest1 = (
        (jnp.sum(jnp.where(ri[:, s, None] == experts, pstarts, 0), axis=-1) + ri[:, TOP_K + s]).astype(jnp.int32)
        for s in range(TOP_K))
    blk_row0 = jnp.arange(N_BLK, dtype=jnp.int32) * ROW_BLOCK
    blk_expert = jnp.minimum(jnp.sum(pends[None, :] <= blk_row0[:, None], axis=1),
                             N_EXPERTS - 1).astype(jnp.int32)
    n_used = (pends[-1:] // ROW_BLOCK).astype(jnp.int32)
    later_used = jnp.logical_and(experts[None, :] > experts[:, None], (pcounts > 0)[None, :])
    next_expert = jnp.min(jnp.where(later_used, experts[None, :], N_EXPERTS), axis=1)
    next_expert = jnp.where(next_expert == N_EXPERTS, experts, next_expert).astype(jnp.int32)

    owner = blk_expert[:, None] == experts[None, :]
    blk_rows = jnp.sum(jnp.where(owner, counts + pstarts, 0), axis=1) - blk_row0
    blk_rows = jnp.where(blk_row0 < pends[-1], jnp.clip(blk_rows, 0, ROW_BLOCK), 0).astype(jnp.int32)

    xs = _sc_dispatch(h, dest0, dest1)
    y = _experts(layer, blk_expert, blk_rows, next_expert, n_used, xs, w_gate, w_up, w_down)
    return _sc_gather(y, dest0, dest1), rg


def kernel(x, norm_mix, norm_ffn, norm_final, ab_w_in, ab_w_out, pool_w, pool_scale, gm_w_in, gm_norm, gm_ws, gm_bs, gm_w_out, router_group_w, router_group_b, router_expert_w, router_expert_b, moe_w_gate, moe_w_up, moe_w_down):
    assert x.shape == (BATCH, SEQ, D_MODEL) and x.dtype == F32
    xf = x.reshape(N_TOK, D_MODEL)
    moe = None
    for layer in range(DEPTH):
        i = layer // 2
        if layer % 2 == 0:
            xf, zp, qkv = _ab_in(xf, moe, norm_mix[layer][None, :], ab_w_in[i].astype(BF16))
            o = _attention(qkv.reshape(BATCH, SEQ, 3 * ATTN_WIDTH))
            xf = _ab_out(zp.reshape(BATCH, SEQ, POOL_WIDTH), o, xf.reshape(BATCH, SEQ, D_MODEL),
                         pool_w[i].astype(BF16), pool_scale[i][None, :],
                         ab_w_out[i].astype(BF16)).reshape(N_TOK, D_MODEL)
        else:
            bs_bcast = jnp.broadcast_to(gm_bs[i][:, :, None], (GM_GROUPS, GM_CHUNK, GM_CHUNK))
            xf = _gmlp(xf, moe, norm_mix[layer][None, :], gm_w_in[i].astype(BF16), gm_norm[i][None, :],
                       gm_ws[i], bs_bcast, gm_w_out[i].astype(BF16))
        moe = _moe(layer, xf, norm_ffn[layer], router_group_w[layer], router_group_b[layer],
                   router_expert_w[layer], router_expert_b[layer], moe_w_gate, moe_w_up, moe_w_down)
    return _combine(moe[0], xf, moe[1], norm_final[None, :]).reshape(BATCH, SEQ, D_MODEL)
```
